```python
import math
import jax, jax.numpy as jnp
from jax import lax
import numpy as np

D_MODEL = 1024
BATCH = 8
SEQ = 4096
DEPTH = 1

CHUNK = 64
CONV_W = 4
EPS = 1e-6

SSD_HEADS = 8
SSD_HEAD_DIM = 64
SSD_WIDTH = SSD_HEADS * SSD_HEAD_DIM
SSD_GROUPS = 2
SSD_STATE = 128
SSD_BC = SSD_GROUPS * SSD_STATE
SSD_CONV_CH = SSD_WIDTH + 2 * SSD_BC

GDN_HEADS = 4
GDN_HEAD_K = 128
GDN_HEAD_V = 128
GDN_KEY_WIDTH = GDN_HEADS * GDN_HEAD_K
GDN_WIDTH = GDN_HEADS * GDN_HEAD_V
GDN_CONV_CH = 2 * GDN_KEY_WIDTH + GDN_WIDTH

MIX_WIDTH = SSD_WIDTH + GDN_WIDTH

OFF_SSD_Z = 0
OFF_SSD_XBC = OFF_SSD_Z + SSD_WIDTH
OFF_SSD_DT = OFF_SSD_XBC + SSD_CONV_CH
OFF_GDN_QKV = OFF_SSD_DT + SSD_HEADS
OFF_GDN_GATE = OFF_GDN_QKV + GDN_CONV_CH
OFF_GDN_BETA = OFF_GDN_GATE + GDN_WIDTH
OFF_GDN_ALPHA = OFF_GDN_BETA + GDN_HEADS
IN_COLS = OFF_GDN_ALPHA + GDN_HEADS

PEER_HEADS = 8
PEER_DK = 256
PEER_HALF = PEER_DK // 2
N_KEYS = 128
N_EXPERTS = N_KEYS * N_KEYS
PEER_TOPK = 16
PEER_BLOCK = 128

kernel_name = 'hymba_ssd_gdn_peer_block'


def rmsnorm(x, w):
    xf = x.astype(jnp.float32)
    y = xf * lax.rsqrt(jnp.mean(xf * xf, axis=-1, keepdims=True) + EPS)
    return (y * w.astype(jnp.float32)).astype(x.dtype)


def l2norm(t):
    return t * lax.rsqrt(jnp.sum(t * t, axis=-1, keepdims=True) + EPS)


def causal_dwconv(u, w):
    c = u.shape[-1]
    return lax.conv_general_dilated(
        u, w[:, None, :].astype(u.dtype), window_strides=(1,),
        padding=[(CONV_W - 1, 0)], dimension_numbers=('NWC', 'WIO', 'NWC'),
        feature_group_count=c)


def ssd_chunked(x, dt, a, b_mat, c_mat):
    bsz, s, h, p = x.shape
    g, n = b_mat.shape[2], b_mat.shape[3]
    r = h // g
    nc = s // CHUNK
    xr = (x * dt[..., None]).reshape(bsz, nc, CHUNK, g, r, p)
    da_cum = jnp.cumsum((dt * a).reshape(bsz, nc, CHUNK, g, r), axis=2)
    br = b_mat.reshape(bsz, nc, CHUNK, g, n)
    cr = c_mat.reshape(bsz, nc, CHUNK, g, n)
    incl = jnp.tril(jnp.ones((CHUNK, CHUNK), dtype=bool))
    seg = da_cum[:, :, :, None] - da_cum[:, :, None, :]
    decay = jnp.exp(jnp.where(incl[:, :, None, None], seg, -jnp.inf))
    cb = jnp.einsum('bclgn,bcsgn->bclsg', cr, br)
    y_diag = jnp.einsum('bclsgr,bcsgrp->bclgrp', cb[..., None] * decay, xr)
    decay_to_end = jnp.exp(da_cum[:, :, -1:] - da_cum)
    states = jnp.einsum('bclgn,bclgrp->bcgrpn', br, xr * decay_to_end[..., None])
    chunk_decay = jnp.exp(da_cum[:, :, -1])

    def step(state, inp):
        dec, st = inp
        return state * dec[..., None, None] + st, state

    _, prev = lax.scan(step, jnp.zeros_like(states[:, 0]),
                       (jnp.moveaxis(chunk_decay, 1, 0), jnp.moveaxis(states, 1, 0)))
    prev = jnp.moveaxis(prev, 0, 1)
    y_off = jnp.einsum('bclgn,bcgrpn->bclgrp', cr, prev) * jnp.exp(da_cum)[..., None]
    return (y_diag + y_off).reshape(bsz, s, h, p)


def gated_delta_chunked(q, k, v, g, beta):
    b, s, h, dk = q.shape
    dv = v.shape[-1]
    nc = s // CHUNK

    def chunks(t):
        return jnp.moveaxis(t, 2, 1).reshape((b, h, nc, CHUNK) + t.shape[3:])

    q = chunks(q) * (dk ** -0.5)
    k = chunks(k)
    v = chunks(v)
    g_cum = jnp.cumsum(chunks(g), axis=-1)
    beta = chunks(beta)
    incl = jnp.tril(jnp.ones((CHUNK, CHUNK), dtype=bool))
    strict = jnp.tril(jnp.ones((CHUNK, CHUNK), dtype=bool), -1)
    decay = jnp.exp(jnp.where(incl, g_cum[..., :, None] - g_cum[..., None, :], -jnp.inf))
    kb = k * beta[..., None]
    a_mat = jnp.where(strict, jnp.einsum('bhcld,bhcsd->bhcls', kb, k) * decay, 0.0)
    eye = jnp.eye(CHUNK, dtype=jnp.float32)
    t_mat = lax.linalg.triangular_solve(a_mat + eye, jnp.broadcast_to(eye, a_mat.shape),
                                        left_side=True, lower=True, unit_diagonal=True)
    u = t_mat @ (v * beta[..., None])
    w = t_mat @ (kb * jnp.exp(g_cum)[..., None])
    qk = jnp.einsum('bhcld,bhcsd->bhcls', q, k) * decay
    q_dec = q * jnp.exp(g_cum)[..., None]
    k_dec = k * jnp.exp(g_cum[..., -1:] - g_cum)[..., None]
    chunk_decay = jnp.exp(g_cum[..., -1])

    def step(state, inp):
        u_c, w_c, qk_c, qd_c, kd_c, dec_c = inp
        v_new = u_c - w_c @ state
        o_c = qd_c @ state + qk_c @ v_new
        state = state * dec_c[..., None, None] + jnp.swapaxes(kd_c, -1, -2) @ v_new
        return state, o_c

    xs = tuple(jnp.moveaxis(t, 2, 0) for t in (u, w, qk, q_dec, k_dec, chunk_decay))
    _, o = lax.scan(step, jnp.zeros((b, h, dk, dv), jnp.float32), xs)
    o = jnp.moveaxis(o, 0, 2).reshape(b, h, s, dv)
    return jnp.moveaxis(o, 1, 2)


def hybrid_mixer(hn, w_in, ssd_conv_w, ssd_conv_b, ssd_dt_bias, ssd_a_log, ssd_d,
                 ssd_norm_w, gdn_conv_w, gdn_dt_bias, gdn_a_log, gdn_norm_w, w_out):
    f32 = jnp.float32
    b, s, _ = hn.shape
    proj = hn @ w_in

    z = proj[..., OFF_SSD_Z:OFF_SSD_XBC].astype(f32)
    xbc = jax.nn.silu(causal_dwconv(proj[..., OFF_SSD_XBC:OFF_SSD_DT], ssd_conv_w)
                      + ssd_conv_b).astype(f32)
    xs = xbc[..., :SSD_WIDTH].reshape(b, s, SSD_HEADS, SSD_HEAD_DIM)
    bm = xbc[..., SSD_WIDTH:SSD_WIDTH + SSD_BC].reshape(b, s, SSD_GROUPS, SSD_STATE)
    cm = xbc[..., SSD_WIDTH + SSD_BC:].reshape(b, s, SSD_GROUPS, SSD_STATE)
    dt = jax.nn.softplus(proj[..., OFF_SSD_DT:OFF_GDN_QKV].astype(f32) + ssd_dt_bias.astype(f32))
    a = -jnp.exp(ssd_a_log.astype(f32))
    y = ssd_chunked(xs, dt, a, bm, cm) + ssd_d.astype(f32)[:, None] * xs
    y_ssd = rmsnorm(y.reshape(b, s, SSD_WIDTH) * jax.nn.silu(z), ssd_norm_w)

    qkv = jax.nn.silu(causal_dwconv(proj[..., OFF_GDN_QKV:OFF_GDN_GATE], gdn_conv_w)).astype(f32)
    q = l2norm(qkv[..., :GDN_KEY_WIDTH].reshape(b, s, GDN_HEADS, GDN_HEAD_K))
    k = l2norm(qkv[..., GDN_KEY_WIDTH:2 * GDN_KEY_WIDTH].reshape(b, s, GDN_HEADS, GDN_HEAD_K))
    v = qkv[..., 2 * GDN_KEY_WIDTH:].reshape(b, s, GDN_HEADS, GDN_HEAD_V)
    beta = jax.nn.sigmoid(proj[..., OFF_GDN_BETA:OFF_GDN_ALPHA].astype(f32))
    g = -jnp.exp(gdn_a_log.astype(f32)) * jax.nn.softplus(
        proj[..., OFF_GDN_ALPHA:IN_COLS].astype(f32) + gdn_dt_bias.astype(f32))
    o = gated_delta_chunked(q, k, v, g, beta)
    gate = proj[..., OFF_GDN_GATE:OFF_GDN_BETA].astype(f32).reshape(b, s, GDN_HEADS, GDN_HEAD_V)
    y_gdn = (rmsnorm(o, gdn_norm_w) * jax.nn.silu(gate)).reshape(b, s, GDN_WIDTH)

    mixed = jnp.concatenate([y_ssd, y_gdn], axis=-1).astype(hn.dtype)
    return mixed @ w_out


def peer_ffn(hn, w_q, sub_keys, u_tab, v_tab):
    f32 = jnp.float32
    b, s, d = hn.shape
    t = b * s
    hf = hn.reshape(t, d)
    q = (hf @ w_q).astype(f32).reshape(t, PEER_HEADS, 2, PEER_HALF)
    scores = jnp.einsum('thpd,pkd->thpk', q, sub_keys.astype(f32))
    s1, i1 = lax.top_k(scores[:, :, 0], PEER_TOPK)
    s2, i2 = lax.top_k(scores[:, :, 1], PEER_TOPK)
    cand = (s1[..., :, None] + s2[..., None, :]).reshape(t, PEER_HEADS, PEER_TOPK * PEER_TOPK)
    top_s, top_c = lax.top_k(cand, PEER_TOPK)
    e1 = jnp.take_along_axis(i1, top_c // PEER_TOPK, axis=-1)
    e2 = jnp.take_along_axis(i2, top_c % PEER_TOPK, axis=-1)
    idx = (e1 * N_KEYS + e2).reshape(t, PEER_HEADS * PEER_TOPK)
    gates = jax.nn.softmax(top_s, axis=-1).reshape(t, PEER_HEADS * PEER_TOPK)
    nb = t // PEER_BLOCK

    def block(args):
        hb, ib, gb = args
        act = jax.nn.gelu(jnp.einsum('td,tkd->tk', hb, u_tab[ib]).astype(f32), approximate=False)
        return jnp.einsum('tk,tkd->td', (gb * act).astype(hb.dtype), v_tab[ib])

    y = lax.map(block, (hf.reshape(nb, PEER_BLOCK, d),
                        idx.reshape(nb, PEER_BLOCK, PEER_HEADS * PEER_TOPK),
                        gates.reshape(nb, PEER_BLOCK, PEER_HEADS * PEER_TOPK)))
    return y.reshape(b, s, d)


def setup_inputs(seed: int = 0) -> dict:
    key = jax.random.key(seed)
    ks = jax.random.split(key, 24)
    f32 = jnp.float32

    def nrm(k, shape, scale):
        return jax.random.normal(k, shape, f32) * scale

    def gain(k, shape):
        return 1.0 + 0.02 * jax.random.normal(k, shape, f32)

    def dt_bias(k, n):
        dt = jnp.exp(jax.random.uniform(k, (DEPTH, n), f32, math.log(1e-3), math.log(1e-1)))
        return dt + jnp.log(-jnp.expm1(-dt))

    def a_log(k, n):
        return jnp.log(jax.random.uniform(k, (DEPTH, n), f32, 1.0, 16.0))

    return {
        'x': nrm(ks[0], (BATCH, SEQ, D_MODEL), 1.0),
        'norm_mix_w': gain(ks[1], (DEPTH, D_MODEL)),
        'w_in': nrm(ks[2], (DEPTH, D_MODEL, IN_COLS), D_MODEL ** -0.5),
        'ssd_conv_w': nrm(ks[3], (DEPTH, CONV_W, SSD_CONV_CH), CONV_W ** -0.5),
        'ssd_conv_b': nrm(ks[4], (DEPTH, SSD_CONV_CH), 0.02),
        'ssd_dt_bias': dt_bias(ks[5], SSD_HEADS),
        'ssd_a_log': a_log(ks[6], SSD_HEADS),
        'ssd_d': gain(ks[7], (DEPTH, SSD_HEADS)),
        'ssd_norm_w': gain(ks[8], (DEPTH, SSD_WIDTH)),
        'gdn_conv_w': nrm(ks[9], (DEPTH, CONV_W, GDN_CONV_CH), CONV_W ** -0.5),
        'gdn_dt_bias': dt_bias(ks[10], GDN_HEADS),
        'gdn_a_log': a_log(ks[11], GDN_HEADS),
        'gdn_norm_w': gain(ks[12], (DEPTH, GDN_HEAD_V)),
        'w_out': nrm(ks[13], (DEPTH, MIX_WIDTH, D_MODEL), MIX_WIDTH ** -0.5),
        'norm_ffn_w': gain(ks[14], (DEPTH, D_MODEL)),
        'peer_w_q': nrm(ks[15], (DEPTH, D_MODEL, PEER_HEADS * PEER_DK), D_MODEL ** -0.5),
        'peer_sub_keys': nrm(ks[16], (DEPTH, 2, N_KEYS, PEER_HALF), PEER_HALF ** -0.5),
        'peer_u': nrm(ks[17], (DEPTH, N_EXPERTS, D_MODEL), D_MODEL ** -0.5),
        'peer_v': nrm(ks[18], (DEPTH, N_EXPERTS, D_MODEL), PEER_HEADS ** -0.5),
        'norm_final_w': gain(ks[19], (D_MODEL,)),
    }


def reference(x, norm_mix_w, w_in, ssd_conv_w, ssd_conv_b, ssd_dt_bias, ssd_a_log, ssd_d,
              ssd_norm_w, gdn_conv_w, gdn_dt_bias, gdn_a_log, gdn_norm_w, w_out,
              norm_ffn_w, peer_w_q, peer_sub_keys, peer_u, peer_v, norm_final_w):
    for l in range(DEPTH):
        x = x + hybrid_mixer(rmsnorm(x, norm_mix_w[l]), w_in[l], ssd_conv_w[l], ssd_conv_b[l],
                             ssd_dt_bias[l], ssd_a_log[l], ssd_d[l], ssd_norm_w[l],
                             gdn_conv_w[l], gdn_dt_bias[l], gdn_a_log[l], gdn_norm_w[l], w_out[l])
        x = x + peer_ffn(rmsnorm(x, norm_ffn_w[l]), peer_w_q[l], peer_sub_keys[l],
                         peer_u[l], peer_v[l])
    return rmsnorm(x, norm_final_w)
```

```python
import functools
import math

import jax
import jax.numpy as jnp
from jax import lax
from jax.experimental import pallas as pl
from jax.experimental.pallas import tpu as pltpu

F32 = jnp.float32
BF16 = jnp.bfloat16
HIGHEST = lax.Precision.HIGHEST

D_MODEL = 1024
CHUNK = 64
CONV_W = 4
EPS = 1e-6

SSD_HEADS = 8
SSD_HEAD_DIM = 64
SSD_WIDTH = SSD_HEADS * SSD_HEAD_DIM
SSD_GROUPS = 2
SSD_STATE = 128
SSD_BC = SSD_GROUPS * SSD_STATE
SSD_CONV_CH = SSD_WIDTH + 2 * SSD_BC
SSD_GROUP_W = SSD_WIDTH // SSD_GROUPS

GDN_HEADS = 4
GDN_HEAD_K = 128
GDN_HEAD_V = 128
GDN_KEY_WIDTH = GDN_HEADS * GDN_HEAD_K
GDN_WIDTH = GDN_HEADS * GDN_HEAD_V
GDN_CONV_CH = 2 * GDN_KEY_WIDTH + GDN_WIDTH

MIX_WIDTH = SSD_WIDTH + GDN_WIDTH

OFF_SSD_Z = 0
OFF_SSD_XBC = OFF_SSD_Z + SSD_WIDTH
OFF_SSD_DT = OFF_SSD_XBC + SSD_CONV_CH
OFF_GDN_QKV = OFF_SSD_DT + SSD_HEADS
OFF_GDN_GATE = OFF_GDN_QKV + GDN_CONV_CH
OFF_GDN_BETA = OFF_GDN_GATE + GDN_WIDTH
OFF_GDN_ALPHA = OFF_GDN_BETA + GDN_HEADS
IN_COLS = OFF_GDN_ALPHA + GDN_HEADS

PEER_HEADS = 8
PEER_DK = 256
PEER_HALF = PEER_DK // 2
N_KEYS = 128
N_EXPERTS = N_KEYS * N_KEYS
PEER_TOPK = 16

LANES = 128
SM_DT = 0
SM_BETA = SM_DT + SSD_HEADS
SM_ALPHA = SM_BETA + GDN_HEADS
BIG_Z = 0
BIG_XBC = BIG_Z + SSD_WIDTH
BIG_QKV = BIG_XBC + SSD_CONV_CH
BIG_GATE = BIG_QKV + GDN_CONV_CH
BIG_COLS = BIG_GATE + GDN_WIDTH

CONV_PAD = 8
VMEM_LIMIT = 56 * 1024 * 1024

NT_DIMS = (((1,), (1,)), ((), ()))
TN_DIMS = (((0,), (0,)), ((), ()))


def _softplus(x):
    return jnp.maximum(x, 0.0) + jnp.log1p(jnp.exp(-jnp.abs(x)))


def _silu(x):
    return x * jax.nn.sigmoid(x)


def _bdot(a, b):
    return jnp.dot(a.astype(BF16), b.astype(BF16), preferred_element_type=F32)


def _bdot_nt(a, b):
    return lax.dot_general(a.astype(BF16), b.astype(BF16), NT_DIMS, preferred_element_type=F32)


def _bdot_tn(a, b):
    return lax.dot_general(a.astype(BF16), b.astype(BF16), TN_DIMS, preferred_element_type=F32)


def _xdot(a, b):
    return jnp.dot(a, b, precision=HIGHEST, preferred_element_type=F32)


def _inproj_kernel(x_ref, nw_ref, wbig_ref, wsm_ref, wsmt_ref, big_ref, small_ref, smallt_ref):
    x = x_ref[...]
    h = x * lax.rsqrt(jnp.mean(x * x, axis=-1, keepdims=True) + EPS) * nw_ref[...]
    hb = h.astype(BF16)
    big_ref[...] = jnp.dot(hb, wbig_ref[...], preferred_element_type=F32)
    small_ref[...] = jnp.dot(hb, wsm_ref[...], preferred_element_type=F32)
    smallt_ref[...] = lax.dot_general(wsmt_ref[...], hb, NT_DIMS, preferred_element_type=F32)


def _inproj(x2, nw, wbig, wsm, wsmt, tm):
    t = x2.shape[0]
    const = lambda i: (0, 0)
    return pl.pallas_call(
        _inproj_kernel,
        grid=(t // tm,),
        in_specs=[
            pl.BlockSpec((tm, D_MODEL), lambda i: (i, 0)),
            pl.BlockSpec((1, D_MODEL), const),
            pl.BlockSpec((D_MODEL, BIG_COLS), const),
            pl.BlockSpec((D_MODEL, LANES), const),
            pl.BlockSpec((LANES, D_MODEL), const),
        ],
        out_specs=[
            pl.BlockSpec((tm, BIG_COLS), lambda i: (i, 0)),
            pl.BlockSpec((tm, LANES), lambda i: (i, 0)),
            pl.BlockSpec((LANES, tm), lambda i: (0, i)),
        ],
        out_shape=[
            jax.ShapeDtypeStruct((t, BIG_COLS), F32),
            jax.ShapeDtypeStruct((t, LANES), F32),
            jax.ShapeDtypeStruct((LANES, t), F32),
        ],
        compiler_params=pltpu.CompilerParams(
            dimension_semantics=("arbitrary",), vmem_limit_bytes=VMEM_LIMIT),
        name="inproj",
    )(x2, nw, wbig, wsm, wsmt)


def _causal_conv(ext_ref, u, w_ref, rows):
    ext_ref[CONV_PAD:CONV_PAD + rows, :] = u
    base = CONV_PAD - (CONV_W - 1)
    acc = w_ref[0:1, :] * ext_ref[base:base + rows, :]
    for j in range(1, CONV_W):
        acc = acc + w_ref[j:j + 1, :] * ext_ref[base + j:base + j + rows, :]
    ext_ref[base:CONV_PAD, :] = ext_ref[rows + base:rows + CONV_PAD, :]
    return acc


def _unit_lower_inverse(n_mat, eye):
    p = eye + n_mat
    nk = n_mat
    for _ in range(int(math.log2(CHUNK)) - 1):
        nk = _xdot(nk, nk)
        p = p + _xdot(p, nk)
    return p


def _mixer_kernel(big_ref, small_ref, smallt_ref, x_ref,
                  scw_ref, scb_ref, gcw_ref, prow_ref, pcol_ref,
                  es_ref, eb_ref, eg_ref, drow_ref, snw_ref, gnw_ref,
                  wout_ref, nfw_ref,
                  x1_ref, hn2_ref,
                  ext_s, ext_g, sstate, gstate, ybuf, obuf):
    rows = x_ref.shape[0]
    n_chunks = rows // CHUNK

    @pl.when(pl.program_id(1) == 0)
    def _():
        ext_s[0:CONV_PAD, :] = jnp.zeros((CONV_PAD, SSD_CONV_CH), F32)
        ext_g[0:CONV_PAD, :] = jnp.zeros((CONV_PAD, GDN_CONV_CH), F32)
        sstate[...] = jnp.zeros(sstate.shape, F32)
        gstate[...] = jnp.zeros(gstate.shape, F32)

    sm = small_ref[...]
    sp = _softplus(sm + prow_ref[0:1, :])
    d_a = sp * prow_ref[1:2, :]
    beta = jax.nn.sigmoid(sm)
    smt = smallt_ref[...]
    spt = _softplus(smt + pcol_ref[:, 0:1])
    d_at = spt * pcol_ref[:, 1:2]

    ri = lax.broadcasted_iota(jnp.int32, (rows, rows), 0)
    ci = lax.broadcasted_iota(jnp.int32, (rows, rows), 1)
    same = (ri >> 6) == (ci >> 6)
    tri_l = jnp.where(same & (ri >= ci), 1.0, 0.0).astype(F32)
    tri_u = jnp.where(same & (ri <= ci), 1.0, 0.0).astype(F32)
    blk = jnp.where(same, 1.0, 0.0).astype(F32)
    cum = _xdot(tri_l, d_a)
    tot = _xdot(blk, d_a)
    cumt = _xdot(d_at, tri_u)
    e_cum = jnp.exp(cum)
    e_end = jnp.exp(tot - cum)

    xw_scale = _xdot(sp * e_end, es_ref[...])
    yoff_scale = _xdot(e_cum, es_ref[...])
    beta_x = _xdot(beta, eb_ref[...])
    egc_x = _xdot(e_cum, eg_ref[...])
    ekd_x = _xdot(e_end, eg_ref[...])

    xbc = _silu(_causal_conv(ext_s, big_ref[:, BIG_XBC:BIG_QKV], scw_ref, rows) + scb_ref[...])
    qkv = _silu(_causal_conv(ext_g, big_ref[:, BIG_QKV:BIG_GATE], gcw_ref, rows))

    xs = xbc[:, :SSD_WIDTH]
    xw = xs * xw_scale

    def l2n(t):
        return t * lax.rsqrt(jnp.sum(t * t, axis=-1, keepdims=True) + EPS)

    q_parts, k_parts = [], []
    for hh in range(GDN_HEADS):
        q_parts.append(l2n(qkv[:, hh * GDN_HEAD_K:(hh + 1) * GDN_HEAD_K]))
        k_parts.append(l2n(qkv[:, GDN_KEY_WIDTH + hh * GDN_HEAD_K:GDN_KEY_WIDTH + (hh + 1) * GDN_HEAD_K]))
    q_all = jnp.concatenate(q_parts, axis=1) * (GDN_HEAD_K ** -0.5)
    k_all = jnp.concatenate(k_parts, axis=1)
    v_all = qkv[:, 2 * GDN_KEY_WIDTH:]
    kb = k_all * beta_x
    vb = v_all * beta_x
    kbg = kb * egc_x
    q_dec = q_all * egc_x
    k_dec = k_all * ekd_x

    li = lax.broadcasted_iota(jnp.int32, (CHUNK, CHUNK), 0)
    si = lax.broadcasted_iota(jnp.int32, (CHUNK, CHUNK), 1)
    incl = li >= si
    strict = li > si
    eye = jnp.where(li == si, 1.0, 0.0).astype(F32)
    neg_inf = jnp.float32(-jnp.inf)

    for c in range(n_chunks):
        r0 = c * CHUNK
        r1 = r0 + CHUNK
        for g in range(SSD_GROUPS):
            c0 = g * SSD_GROUP_W
            b_m = xbc[r0:r1, SSD_WIDTH + g * SSD_STATE:SSD_WIDTH + (g + 1) * SSD_STATE]
            c_m = xbc[r0:r1, SSD_WIDTH + SSD_BC + g * SSD_STATE:SSD_WIDTH + SSD_BC + (g + 1) * SSD_STATE]
            cb = _bdot_nt(c_m, b_m)
            for hl in range(SSD_HEADS // SSD_GROUPS):
                hh = g * (SSD_HEADS // SSD_GROUPS) + hl
                col = SM_DT + hh
                seg = cum[r0:r1, col:col + 1] - cumt[col:col + 1, r0:r1]
                dec = jnp.exp(jnp.where(incl, seg, neg_inf))
                m_h = cb * dec * spt[col:col + 1, r0:r1]
                x_h = xs[r0:r1, hh * SSD_HEAD_DIM:(hh + 1) * SSD_HEAD_DIM]
                ybuf[r0:r1, hh * SSD_HEAD_DIM:(hh + 1) * SSD_HEAD_DIM] = _bdot(m_h, x_h)
            st = sstate[g]
            y_off = _bdot(c_m, st) * yoff_scale[r0:r1, c0:c0 + SSD_GROUP_W]
            ybuf[r0:r1, c0:c0 + SSD_GROUP_W] = (ybuf[r0:r1, c0:c0 + SSD_GROUP_W] + y_off
                                                + drow_ref[:, c0:c0 + SSD_GROUP_W] * xs[r0:r1, c0:c0 + SSD_GROUP_W])
            s_new = _bdot_tn(b_m, xw[r0:r1, c0:c0 + SSD_GROUP_W])
            sstate[g] = st * yoff_scale[r1 - 1:r1, c0:c0 + SSD_GROUP_W] + s_new
        for hh in range(GDN_HEADS):
            k0 = hh * GDN_HEAD_K
            k1 = k0 + GDN_HEAD_K
            col = SM_ALPHA + hh
            seg = cum[r0:r1, col:col + 1] - cumt[col:col + 1, r0:r1]
            dec = jnp.exp(jnp.where(incl, seg, neg_inf))
            k_c = k_all[r0:r1, k0:k1]
            a_mat = jnp.where(strict, _bdot_nt(kb[r0:r1, k0:k1], k_c) * dec, 0.0)
            t_mat = _unit_lower_inverse(-a_mat, eye)
            u_c = _bdot(t_mat, vb[r0:r1, k0:k1])
            w_c = _bdot(t_mat, kbg[r0:r1, k0:k1])
            qk = _bdot_nt(q_all[r0:r1, k0:k1], k_c) * dec
            state = gstate[hh]
            v_new = u_c - _bdot(w_c, state)
            obuf[r0:r1, k0:k1] = _bdot(q_dec[r0:r1, k0:k1], state) + _bdot(qk, v_new)
            gstate[hh] = state * egc_x[r1 - 1:r1, k0:k1] + _bdot_tn(k_dec[r0:r1, k0:k1], v_new)

    z = big_ref[:, BIG_Z:BIG_XBC]
    t = ybuf[...] * _silu(z)
    y_ssd = t * lax.rsqrt(jnp.mean(t * t, axis=-1, keepdims=True) + EPS) * snw_ref[...]
    o = obuf[...]
    gate = big_ref[:, BIG_GATE:BIG_COLS]
    parts = [y_ssd]
    for hh in range(GDN_HEADS):
        k0 = hh * GDN_HEAD_V
        o_h = o[:, k0:k0 + GDN_HEAD_V]
        parts.append(o_h * lax.rsqrt(jnp.mean(o_h * o_h, axis=-1, keepdims=True) + EPS)
                     * gnw_ref[...] * _silu(gate[:, k0:k0 + GDN_HEAD_V]))
    mixed = jnp.concatenate(parts, axis=1).astype(BF16)
    x1 = x_ref[...] + jnp.dot(mixed, wout_ref[...], preferred_element_type=F32)
    x1_ref[...] = x1
    hn2 = x1 * lax.rsqrt(jnp.mean(x1 * x1, axis=-1, keepdims=True) + EPS) * nfw_ref[...]
    hn2_ref[...] = hn2.astype(BF16)


def _mixer(big, small, smallt, x2, scw, scb, gcw, prow, pcol, e_s, e_b, e_g, drow, snw, gnw,
           wout, nfw, batch, seq, rows):
    t = x2.shape[0]
    nblk = seq // rows
    tok = lambda b, s: (b * nblk + s, 0)
    tokt = lambda b, s: (0, b * nblk + s)
    const = lambda b, s: (0, 0)

    def full(a):
        return pl.BlockSpec(a.shape, const)

    return pl.pallas_call(
        _mixer_kernel,
        grid=(batch, nblk),
        in_specs=[
            pl.BlockSpec((rows, BIG_COLS), tok),
            pl.BlockSpec((rows, LANES), tok),
            pl.BlockSpec((LANES, rows), tokt),
            pl.BlockSpec((rows, D_MODEL), tok),
            full(scw), full(scb), full(gcw), full(prow), full(pcol),
            full(e_s), full(e_b), full(e_g), full(drow), full(snw), full(gnw),
            full(wout), full(nfw),
        ],
        out_specs=[
            pl.BlockSpec((rows, D_MODEL), tok),
            pl.BlockSpec((rows, D_MODEL), tok),
        ],
        out_shape=[
            jax.ShapeDtypeStruct((t, D_MODEL), F32),
            jax.ShapeDtypeStruct((t, D_MODEL), BF16),
        ],
        scratch_shapes=[
            pltpu.VMEM((rows + CONV_PAD, SSD_CONV_CH), F32),
            pltpu.VMEM((rows + CONV_PAD, GDN_CONV_CH), F32),
            pltpu.VMEM((SSD_GROUPS, SSD_STATE, SSD_GROUP_W), F32),
            pltpu.VMEM((GDN_HEADS, GDN_HEAD_K, GDN_HEAD_V), F32),
            pltpu.VMEM((rows, SSD_WIDTH), F32),
            pltpu.VMEM((rows, GDN_WIDTH), F32),
        ],
        compiler_params=pltpu.CompilerParams(
            dimension_semantics=("arbitrary", "arbitrary"), vmem_limit_bytes=VMEM_LIMIT),
        name="mixer",
    )(big, small, smallt, x2, scw, scb, gcw, prow, pcol, e_s, e_b, e_g, drow, snw, gnw, wout, nfw)


def _top_desc(s, count):
    neg_inf = jnp.float32(-jnp.inf)
    vals = []
    for _ in range(count):
        m = jnp.max(s, axis=0, keepdims=True)
        vals.append(m)
        s = jnp.where(s == m, neg_inf, s)
    return jnp.concatenate(vals, axis=0)


def _route_kernel(hn2_ref, wqt_ref, keys_ref, s1_ref, s2_ref, e1_ref, e2_ref, tau_ref, qt_scr):
    tn = hn2_ref.shape[0]
    qt_scr[...] = lax.dot_general(wqt_ref[...], hn2_ref[...], NT_DIMS, preferred_element_type=F32)
    neg_inf = jnp.float32(-jnp.inf)
    sub8 = lax.broadcasted_iota(jnp.int32, (8, LANES), 0)

    def head_body(h, carry):
        q0 = pl.multiple_of(h * PEER_DK, PEER_DK)
        s1_all = _bdot(keys_ref[0], qt_scr[pl.ds(q0, PEER_HALF), :])
        s2_all = _bdot(keys_ref[1], qt_scr[pl.ds(q0 + PEER_HALF, PEER_HALF), :])
        tau_rows = []
        for lg in range(tn // LANES):
            l0 = lg * LANES
            s1 = s1_all[:, l0:l0 + LANES]
            s2 = s2_all[:, l0:l0 + LANES]
            a = _top_desc(s1, PEER_TOPK)
            b = _top_desc(s2, PEER_TOPK)
            cands = [b + a[0:1], b[0:8] + a[1:2]]
            for i in range(2, 8):
                cands.append(jnp.where(sub8 < (PEER_TOPK // (i + 1)), b[0:8] + a[i:i + 1], neg_inf))
            cands.append(a[8:16] + b[0:1])
            cand = jnp.concatenate(cands, axis=0)
            tau = _top_desc(cand, PEER_TOPK)[PEER_TOPK - 1:PEER_TOPK]
            top = a[0:1] + b[0:1]
            zsum = jnp.sum(jnp.where(cand >= tau, jnp.exp(cand - top), 0.0), axis=0, keepdims=True)
            s1_ref[h, :, l0:l0 + LANES] = s1
            s2_ref[h, :, l0:l0 + LANES] = s2
            e1_ref[h, :, l0:l0 + LANES] = jnp.exp(s1 - a[0:1]) / zsum
            e2_ref[h, :, l0:l0 + LANES] = jnp.exp(s2 - b[0:1])
            tau_rows.append(tau)
        tau_ref[pl.ds(h, 1), :] = jnp.concatenate(tau_rows, axis=1)
        return carry

    lax.fori_loop(0, PEER_HEADS, head_body, 0)


def _route(hn2, wqt, keys, tn):
    t = hn2.shape[0]
    sc_spec = pl.BlockSpec((PEER_HEADS, N_KEYS, tn), lambda i: (0, 0, i))
    sc_shape = jax.ShapeDtypeStruct((PEER_HEADS, N_KEYS, t), F32)
    return pl.pallas_call(
        _route_kernel,
        grid=(t // tn,),
        in_specs=[
            pl.BlockSpec((tn, D_MODEL), lambda i: (i, 0)),
            pl.BlockSpec(wqt.shape, lambda i: (0, 0)),
            pl.BlockSpec(keys.shape, lambda i: (0, 0, 0)),
        ],
        out_specs=[sc_spec, sc_spec, sc_spec, sc_spec,
                   pl.BlockSpec((PEER_HEADS, tn), lambda i: (0, i))],
        out_shape=[sc_shape, sc_shape, sc_shape, sc_shape,
                   jax.ShapeDtypeStruct((PEER_HEADS, t), F32)],
        scratch_shapes=[pltpu.VMEM((PEER_HEADS * PEER_DK, tn), F32)],
        compiler_params=pltpu.CompilerParams(
            dimension_semantics=("arbitrary",), vmem_limit_bytes=VMEM_LIMIT),
        name="route",
    )(hn2, wqt, keys)


PEER_ROWS = 16


def _peer_kernel(hn2_ref, u_ref, vt_ref, s1_ref, s2_ref, e1_ref, e2_ref, tau_ref,
                 yt_ref, act_scr, w_scr):
    j = pl.program_id(1)
    te = u_ref.shape[0]
    tm = hn2_ref.shape[0]

    @pl.when(j == 0)
    def _():
        yt_ref[...] = jnp.zeros(yt_ref.shape, F32)

    act_scr[...] = lax.dot_general(u_ref[...], hn2_ref[...], NT_DIMS, preferred_element_type=F32)

    n_e1 = te // N_KEYS
    for e1l in range(n_e1):
        e1 = j * n_e1 + e1l

        def row_body(r, carry):
            r0 = pl.multiple_of(r * PEER_ROWS, PEER_ROWS)
            acc = jnp.zeros((PEER_ROWS, tm), F32)
            for h in range(PEER_HEADS):
                ssum = s2_ref[h, pl.ds(r0, PEER_ROWS), :] + s1_ref[h, pl.ds(e1, 1), :]
                val = e2_ref[h, pl.ds(r0, PEER_ROWS), :] * e1_ref[h, pl.ds(e1, 1), :]
                acc = acc + jnp.where(ssum >= tau_ref[h:h + 1, :], val, 0.0)
            a = act_scr[pl.ds(e1l * N_KEYS + r0, PEER_ROWS), :]
            gelu = 0.5 * a * (1.0 + lax.erf(a * (2.0 ** -0.5)))
            w_scr[pl.ds(e1l * N_KEYS + r0, PEER_ROWS), :] = (acc * gelu).astype(BF16)
            return carry

        lax.fori_loop(0, N_KEYS // PEER_ROWS, row_body, 0)

    yt_ref[...] += jnp.dot(vt_ref[...], w_scr[...], preferred_element_type=F32)


def _peer(hn2, u_bf, vt_bf, s1, s2, e1, e2, tau, tm, te):
    t = hn2.shape[0]
    sc_spec = pl.BlockSpec((PEER_HEADS, N_KEYS, tm), lambda i, j: (0, 0, i))
    return pl.pallas_call(
        _peer_kernel,
        grid=(t // tm, N_EXPERTS // te),
        in_specs=[
            pl.BlockSpec((tm, D_MODEL), lambda i, j: (i, 0)),
            pl.BlockSpec((te, D_MODEL), lambda i, j: (j, 0)),
            pl.BlockSpec((D_MODEL, te), lambda i, j: (0, j)),
            sc_spec, sc_spec, sc_spec, sc_spec,
            pl.BlockSpec((PEER_HEADS, tm), lambda i, j: (0, i)),
        ],
        out_specs=pl.BlockSpec((D_MODEL, tm), lambda i, j: (0, i)),
        out_shape=jax.ShapeDtypeStruct((D_MODEL, t), F32),
        scratch_shapes=[pltpu.VMEM((te, tm), F32), pltpu.VMEM((te, tm), BF16)],
        compiler_params=pltpu.CompilerParams(
            dimension_semantics=("arbitrary", "arbitrary"), vmem_limit_bytes=VMEM_LIMIT),
        name="peer",
    )(hn2, u_bf, vt_bf, s1, s2, e1, e2, tau)


def _final_kernel(x1_ref, yt_ref, w_ref, o_ref):
    x = x1_ref[...] + yt_ref[...].T
    o_ref[...] = x * lax.rsqrt(jnp.mean(x * x, axis=-1, keepdims=True) + EPS) * w_ref[...]


def _final(x1, yt, w, tm):
    t = x1.shape[0]
    return pl.pallas_call(
        _final_kernel,
        grid=(t // tm,),
        in_specs=[
            pl.BlockSpec((tm, D_MODEL), lambda i: (i, 0)),
            pl.BlockSpec((D_MODEL, tm), lambda i: (0, i)),
            pl.BlockSpec((1, D_MODEL), lambda i: (0, 0)),
        ],
        out_specs=pl.BlockSpec((tm, D_MODEL), lambda i: (i, 0)),
        out_shape=jax.ShapeDtypeStruct((t, D_MODEL), F32),
        compiler_params=pltpu.CompilerParams(
            dimension_semantics=("arbitrary",), vmem_limit_bytes=VMEM_LIMIT),
        name="final",
    )(x1, yt, w)


def _pick(n, pref):
    t = min(pref, n)
    while n % t:
        t //= 2
    return t


def _expansion(src_col0, n_heads, width):
    r = jnp.arange(LANES)[:, None]
    c = jnp.arange(n_heads * width)[None, :]
    return (r == src_col0 + c // width).astype(F32)


def _layer(x2, batch, seq, norm_mix_w, w_in, ssd_conv_w, ssd_conv_b, ssd_dt_bias, ssd_a_log, ssd_d,
           ssd_norm_w, gdn_conv_w, gdn_dt_bias, gdn_a_log, gdn_norm_w, w_out, norm_ffn_w,
           peer_w_q, peer_sub_keys, peer_u, peer_v):
    t = x2.shape[0]
    wbig = jnp.concatenate([w_in[:, OFF_SSD_Z:OFF_SSD_DT], w_in[:, OFF_GDN_QKV:OFF_GDN_BETA]],
                           axis=1).astype(BF16)
    wsm = jnp.concatenate([w_in[:, OFF_SSD_DT:OFF_GDN_QKV], w_in[:, OFF_GDN_BETA:IN_COLS]], axis=1)
    wsm = jnp.pad(wsm, ((0, 0), (0, LANES - wsm.shape[1]))).astype(BF16)
    zeros4 = jnp.zeros((GDN_HEADS,), F32)
    bias = jnp.concatenate([ssd_dt_bias.astype(F32), zeros4, gdn_dt_bias.astype(F32)])
    amul = jnp.concatenate([-jnp.exp(ssd_a_log.astype(F32)), zeros4, -jnp.exp(gdn_a_log.astype(F32))])
    pad = LANES - bias.shape[0]
    prow = jnp.pad(jnp.stack([bias, amul]), ((0, 6), (0, pad)))
    pcol = jnp.pad(jnp.stack([bias, amul], axis=1), ((0, pad), (0, LANES - 2)))
    e_s = _expansion(SM_DT, SSD_HEADS, SSD_HEAD_DIM)
    e_b = _expansion(SM_BETA, GDN_HEADS, GDN_HEAD_V)
    e_g = _expansion(SM_ALPHA, GDN_HEADS, GDN_HEAD_V)
    drow = jnp.repeat(ssd_d.astype(F32), SSD_HEAD_DIM)[None, :]

    tm_in = _pick(t, 512)
    big, small, smallt = _inproj(x2, norm_mix_w[None, :], wbig, wsm, wsm.T, tm_in)

    rows = _pick(seq, 256)
    x1, hn2 = _mixer(big, small, smallt, x2, ssd_conv_w.astype(F32), ssd_conv_b[None, :].astype(F32),
                     gdn_conv_w.astype(F32), prow, pcol, e_s, e_b, e_g, drow,
                     ssd_norm_w[None, :].astype(F32), gdn_norm_w[None, :].astype(F32),
                     w_out.astype(BF16), norm_ffn_w[None, :].astype(F32), batch, seq, rows)

    s1, s2, e1, e2, tau = _route(hn2, peer_w_q.T.astype(BF16), peer_sub_keys.astype(BF16), _pick(t, 256))
    yt = _peer(hn2, peer_u.astype(BF16), peer_v.T.astype(BF16), s1, s2, e1, e2, tau,
               _pick(t, 512), 512)
    return x1, yt


def kernel(x, norm_mix_w, w_in, ssd_conv_w, ssd_conv_b, ssd_dt_bias, ssd_a_log, ssd_d, ssd_norm_w,
           gdn_conv_w, gdn_dt_bias, gdn_a_log, gdn_norm_w, w_out, norm_ffn_w, peer_w_q, peer_sub_keys,
           peer_u, peer_v, norm_final_w):
    batch, seq, _ = x.shape
    assert w_in.shape[0] == 1, "single trunk layer"
    x2 = x.reshape(batch * seq, D_MODEL)
    x1, yt = _layer(x2, batch, seq, norm_mix_w[0], w_in[0], ssd_conv_w[0], ssd_conv_b[0],
                    ssd_dt_bias[0], ssd_a_log[0], ssd_d[0], ssd_norm_w[0], gdn_conv_w[0],
                    gdn_dt_bias[0], gdn_a_log[0], gdn_norm_w[0], w_out[0], norm_ffn_w[0],
                    peer_w_q[0], peer_sub_keys[0], peer_u[0], peer_v[0])
    out = _final(x1, yt, norm_final_w[None, :].astype(F32), _pick(batch * seq, 512))
    return out.reshape(batch, seq, D_MODEL)
```

```python
import math

import jax
import jax.numpy as jnp
from jax import lax
from jax.experimental import pallas as pl
from jax.experimental.pallas import tpu as pltpu

F32 = jnp.float32
BF16 = jnp.bfloat16

D_MODEL = 1024
CHUNK = 64
CONV_W = 4
EPS = 1e-6

SSD_HEADS = 8
SSD_HEAD_DIM = 64
SSD_WIDTH = SSD_HEADS * SSD_HEAD_DIM
SSD_GROUPS = 2
SSD_STATE = 128
SSD_BC = SSD_GROUPS * SSD_STATE
SSD_CONV_CH = SSD_WIDTH + 2 * SSD_BC
SSD_GROUP_W = SSD_WIDTH // SSD_GROUPS
SSD_GROUP_HEADS = SSD_HEADS // SSD_GROUPS

GDN_HEADS = 4
GDN_HEAD_K = 128
GDN_HEAD_V = 128
GDN_KEY_WIDTH = GDN_HEADS * GDN_HEAD_K
GDN_WIDTH = GDN_HEADS * GDN_HEAD_V
GDN_CONV_CH = 2 * GDN_KEY_WIDTH + GDN_WIDTH

MIX_WIDTH = SSD_WIDTH + GDN_WIDTH

OFF_SSD_Z = 0
OFF_SSD_XBC = OFF_SSD_Z + SSD_WIDTH
OFF_SSD_DT = OFF_SSD_XBC + SSD_CONV_CH
OFF_GDN_QKV = OFF_SSD_DT + SSD_HEADS
OFF_GDN_GATE = OFF_GDN_QKV + GDN_CONV_CH
OFF_GDN_BETA = OFF_GDN_GATE + GDN_WIDTH
OFF_GDN_ALPHA = OFF_GDN_BETA + GDN_HEADS
IN_COLS = OFF_GDN_ALPHA + GDN_HEADS

PEER_HEADS = 8
PEER_DK = 256
PEER_HALF = PEER_DK // 2
N_KEYS = 128
N_EXPERTS = N_KEYS * N_KEYS
PEER_TOPK = 16

LANES = 128
SUBLANES = 8
SM_DT = 0
SM_BETA = SM_DT + SSD_HEADS
SM_ALPHA = SM_BETA + GDN_HEADS
BIG_Z = 0
BIG_XBC = BIG_Z + SSD_WIDTH
BIG_QKV = BIG_XBC + SSD_CONV_CH
BIG_GATE = BIG_QKV + GDN_CONV_CH
BIG_COLS = BIG_GATE + GDN_WIDTH

CONV_PAD = 8
VMEM_LIMIT = 56 * 1024 * 1024

NT_DIMS = (((1,), (1,)), ((), ()))
TN_DIMS = (((0,), (0,)), ((), ()))


def _softplus(x):
    return jnp.maximum(x, 0.0) + jnp.log1p(jnp.exp(-jnp.abs(x)))


def _silu(x):
    return x * jax.nn.sigmoid(x)


def _bdot(a, b):
    return jnp.dot(a.astype(BF16), b.astype(BF16), preferred_element_type=F32)


def _bdot_nt(a, b):
    return lax.dot_general(a.astype(BF16), b.astype(BF16), NT_DIMS, preferred_element_type=F32)


def _bdot_tn(a, b):
    return lax.dot_general(a.astype(BF16), b.astype(BF16), TN_DIMS, preferred_element_type=F32)


def _split3(x):
    hi = x.astype(BF16)
    r = x - hi.astype(F32)
    mid = r.astype(BF16)
    lo = (r - mid.astype(F32)).astype(BF16)
    return hi, mid, lo


def _sel_dot(x, m01):
    hi, mid, lo = _split3(x)
    d = lambda a: jnp.dot(a, m01, preferred_element_type=F32)
    return d(hi) + d(mid) + d(lo)


def _sel_dot_l(m01, x):
    hi, mid, lo = _split3(x)
    d = lambda a: jnp.dot(m01, a, preferred_element_type=F32)
    return d(hi) + d(mid) + d(lo)


def _inproj_kernel(x_ref, nw_ref, wbig_ref, wsm_ref, wsmt_ref, big_ref, small_ref, smallt_ref):
    x = x_ref[...]
    h = x * lax.rsqrt(jnp.mean(x * x, axis=-1, keepdims=True) + EPS) * nw_ref[...]
    hb = h.astype(BF16)
    big_ref[...] = jnp.dot(hb, wbig_ref[...], preferred_element_type=F32)
    small_ref[...] = jnp.dot(hb, wsm_ref[...], preferred_element_type=F32)
    smallt_ref[...] = lax.dot_general(wsmt_ref[...], hb, NT_DIMS, preferred_element_type=F32)


def _inproj(x2, nw, wbig, wsm, wsmt, tm):
    t = x2.shape[0]
    const = lambda i: (0, 0)
    return pl.pallas_call(
        _inproj_kernel,
        grid=(t // tm,),
        in_specs=[
            pl.BlockSpec((tm, D_MODEL), lambda i: (i, 0)),
            pl.BlockSpec((1, D_MODEL), const),
            pl.BlockSpec((D_MODEL, BIG_COLS), const),
            pl.BlockSpec((D_MODEL, LANES), const),
            pl.BlockSpec((LANES, D_MODEL), const),
        ],
        out_specs=[
            pl.BlockSpec((tm, BIG_COLS), lambda i: (i, 0)),
            pl.BlockSpec((tm, LANES), lambda i: (i, 0)),
            pl.BlockSpec((LANES, tm), lambda i: (0, i)),
        ],
        out_shape=[
            jax.ShapeDtypeStruct((t, BIG_COLS), F32),
            jax.ShapeDtypeStruct((t, LANES), F32),
            jax.ShapeDtypeStruct((LANES, t), F32),
        ],
        compiler_params=pltpu.CompilerParams(
            dimension_semantics=("arbitrary",), vmem_limit_bytes=VMEM_LIMIT),
        name="inproj",
    )(x2, nw, wbig, wsm, wsmt)


def _causal_conv(ext_ref, u, w_ref, rows):
    ext_ref[CONV_PAD:CONV_PAD + rows, :] = u
    base = CONV_PAD - (CONV_W - 1)
    acc = w_ref[0:1, :] * ext_ref[base:base + rows, :]
    for j in range(1, CONV_W):
        acc = acc + w_ref[j:j + 1, :] * ext_ref[base + j:base + j + rows, :]
    ext_ref[base:CONV_PAD, :] = ext_ref[rows + base:rows + CONV_PAD, :]
    return acc


def _unit_lower_inverses(n_mats, eye):
    ps = [eye + n for n in n_mats]
    nks = list(n_mats)
    for _ in range(int(math.log2(CHUNK)) - 1):
        nks = [_bdot(nk, nk) for nk in nks]
        ps = [p + _bdot(p, nk) for p, nk in zip(ps, nks)]
    return ps


def _mixer_kernel(big_ref, small_ref, smallt_ref, x_ref,
                  scw_ref, scb_ref, gcw_ref, prow_ref, pcol_ref,
                  tril_ref, triu_ref, blk_ref, es_ref, eb_ref, eg_ref,
                  drow_ref, snw_ref, gnw_ref, wout_ref, nfw_ref,
                  x1_ref, hn2_ref,
                  ext_s, ext_g, sstate, gstate, ybuf, obuf):
    rows = x_ref.shape[0]
    n_chunks = rows // CHUNK

    @pl.when(pl.program_id(1) == 0)
    def _():
        ext_s[0:CONV_PAD, :] = jnp.zeros((CONV_PAD, SSD_CONV_CH), F32)
        ext_g[0:CONV_PAD, :] = jnp.zeros((CONV_PAD, GDN_CONV_CH), F32)
        sstate[...] = jnp.zeros(sstate.shape, F32)
        gstate[...] = jnp.zeros(gstate.shape, F32)

    sm = small_ref[...]
    sp = _softplus(sm + prow_ref[0:1, :])
    d_a = sp * prow_ref[1:2, :]
    beta = jax.nn.sigmoid(sm)
    smt = smallt_ref[...]
    spt = _softplus(smt + pcol_ref[:, 0:1])
    d_at = spt * pcol_ref[:, 1:2]

    cum = _sel_dot_l(tril_ref[...], d_a)
    tot = _sel_dot_l(blk_ref[...], d_a)
    cumt = _sel_dot(d_at, triu_ref[...])
    e_cum = jnp.exp(cum)
    e_end = jnp.exp(tot - cum)

    xw_scale = _sel_dot(sp * e_end, es_ref[...])
    yoff_scale = _sel_dot(e_cum, es_ref[...])
    beta_x = _sel_dot(beta, eb_ref[...])
    egc_x = _sel_dot(e_cum, eg_ref[...])
    ekd_x = _sel_dot(e_end, eg_ref[...])

    xbc = _silu(_causal_conv(ext_s, big_ref[:, BIG_XBC:BIG_QKV], scw_ref, rows) + scb_ref[...])
    qkv = _silu(_causal_conv(ext_g, big_ref[:, BIG_QKV:BIG_GATE], gcw_ref, rows))

    xs = xbc[:, :SSD_WIDTH]
    xw = xs * xw_scale

    def l2n(t):
        return t * lax.rsqrt(jnp.sum(t * t, axis=-1, keepdims=True) + EPS)

    q_parts, k_parts = [], []
    for hh in range(GDN_HEADS):
        q_parts.append(l2n(qkv[:, hh * GDN_HEAD_K:(hh + 1) * GDN_HEAD_K]))
        k_parts.append(l2n(qkv[:, GDN_KEY_WIDTH + hh * GDN_HEAD_K:GDN_KEY_WIDTH + (hh + 1) * GDN_HEAD_K]))
    q_all = jnp.concatenate(q_parts, axis=1) * (GDN_HEAD_K ** -0.5)
    k_all = jnp.concatenate(k_parts, axis=1)
    v_all = qkv[:, 2 * GDN_KEY_WIDTH:]
    kb = k_all * beta_x
    vb = v_all * beta_x
    kbg = kb * egc_x
    q_dec = q_all * egc_x
    k_dec = k_all * ekd_x

    li = lax.broadcasted_iota(jnp.int32, (CHUNK, CHUNK), 0)
    si = lax.broadcasted_iota(jnp.int32, (CHUNK, CHUNK), 1)
    incl = li >= si
    strict = li > si
    eye = jnp.where(li == si, 1.0, 0.0).astype(F32)
    neg_inf = jnp.float32(-jnp.inf)

    def decay(col, r0):
        seg = cum[r0:r0 + CHUNK, col:col + 1] - cumt[col:col + 1, r0:r0 + CHUNK]
        return jnp.exp(jnp.where(incl, seg, neg_inf))

    chunks = [(c, c * CHUNK) for c in range(n_chunks)]

    b_ms, c_ms = {}, {}
    for c, r0 in chunks:
        for g in range(SSD_GROUPS):
            b_ms[c, g] = xbc[r0:r0 + CHUNK, SSD_WIDTH + g * SSD_STATE:SSD_WIDTH + (g + 1) * SSD_STATE]
            c_ms[c, g] = xbc[r0:r0 + CHUNK,
                             SSD_WIDTH + SSD_BC + g * SSD_STATE:SSD_WIDTH + SSD_BC + (g + 1) * SSD_STATE]
    cbs = {k: _bdot_nt(c_ms[k], b_ms[k]) for k in b_ms}
    for c, r0 in chunks:
        for hh in range(SSD_HEADS):
            col = SM_DT + hh
            m_h = cbs[c, hh // SSD_GROUP_HEADS] * decay(col, r0) * spt[col:col + 1, r0:r0 + CHUNK]
            h0 = hh * SSD_HEAD_DIM
            ybuf[r0:r0 + CHUNK, h0:h0 + SSD_HEAD_DIM] = _bdot(m_h, xs[r0:r0 + CHUNK, h0:h0 + SSD_HEAD_DIM])

    decs, kqs = {}, {}
    for c, r0 in chunks:
        for hh in range(GDN_HEADS):
            k0 = hh * GDN_HEAD_K
            decs[c, hh] = decay(SM_ALPHA + hh, r0)
            lhs = jnp.concatenate([kb[r0:r0 + CHUNK, k0:k0 + GDN_HEAD_K],
                                   q_all[r0:r0 + CHUNK, k0:k0 + GDN_HEAD_K]], axis=0)
            kqs[c, hh] = _bdot_nt(lhs, k_all[r0:r0 + CHUNK, k0:k0 + GDN_HEAD_K])
    keys = list(decs)
    n_mats = [jnp.where(strict, -(kqs[k][:CHUNK] * decs[k]), 0.0) for k in keys]
    t_mats = dict(zip(keys, _unit_lower_inverses(n_mats, eye)))
    uws = {}
    for c, r0 in chunks:
        for hh in range(GDN_HEADS):
            k0 = hh * GDN_HEAD_K
            rhs = jnp.concatenate([vb[r0:r0 + CHUNK, k0:k0 + GDN_HEAD_K],
                                   kbg[r0:r0 + CHUNK, k0:k0 + GDN_HEAD_K]], axis=1)
            uws[c, hh] = _bdot(t_mats[c, hh], rhs)

    for c, r0 in chunks:
        r1 = r0 + CHUNK
        for g in range(SSD_GROUPS):
            c0 = g * SSD_GROUP_W
            st = sstate[g]
            y_off = _bdot(c_ms[c, g], st) * yoff_scale[r0:r1, c0:c0 + SSD_GROUP_W]
            ybuf[r0:r1, c0:c0 + SSD_GROUP_W] = (ybuf[r0:r1, c0:c0 + SSD_GROUP_W] + y_off
                                                + drow_ref[:, c0:c0 + SSD_GROUP_W] * xs[r0:r1, c0:c0 + SSD_GROUP_W])
            s_new = _bdot_tn(b_ms[c, g], xw[r0:r1, c0:c0 + SSD_GROUP_W])
            sstate[g] = st * yoff_scale[r1 - 1:r1, c0:c0 + SSD_GROUP_W] + s_new
        for hh in range(GDN_HEADS):
            k0 = hh * GDN_HEAD_K
            k1 = k0 + GDN_HEAD_K
            state = gstate[hh]
            lhs = jnp.concatenate([uws[c, hh][:, GDN_HEAD_V:], q_dec[r0:r1, k0:k1]], axis=0)
            ws_qs = _bdot(lhs, state)
            v_new = uws[c, hh][:, :GDN_HEAD_V] - ws_qs[:CHUNK]
            qk = kqs[c, hh][CHUNK:] * decs[c, hh]
            obuf[r0:r1, k0:k1] = ws_qs[CHUNK:] + _bdot(qk, v_new)
            gstate[hh] = state * egc_x[r1 - 1:r1, k0:k1] + _bdot_tn(k_dec[r0:r1, k0:k1], v_new)

    z = big_ref[:, BIG_Z:BIG_XBC]
    t = ybuf[...] * _silu(z)
    y_ssd = t * lax.rsqrt(jnp.mean(t * t, axis=-1, keepdims=True) + EPS) * snw_ref[...]
    o = obuf[...]
    gate = big_ref[:, BIG_GATE:BIG_COLS]
    parts = [y_ssd]
    for hh in range(GDN_HEADS):
        k0 = hh * GDN_HEAD_V
        o_h = o[:, k0:k0 + GDN_HEAD_V]
        parts.append(o_h * lax.rsqrt(jnp.mean(o_h * o_h, axis=-1, keepdims=True) + EPS)
                     * gnw_ref[...] * _silu(gate[:, k0:k0 + GDN_HEAD_V]))
    mixed = jnp.concatenate(parts, axis=1).astype(BF16)
    x1 = x_ref[...] + jnp.dot(mixed, wout_ref[...], preferred_element_type=F32)
    x1_ref[...] = x1
    hn2 = x1 * lax.rsqrt(jnp.mean(x1 * x1, axis=-1, keepdims=True) + EPS) * nfw_ref[...]
    hn2_ref[...] = hn2.astype(BF16)


def _mixer(big, small, smallt, x2, consts, batch, seq, rows):
    t = x2.shape[0]
    nblk = seq // rows
    tok = lambda b, s: (b * nblk + s, 0)
    tokt = lambda b, s: (0, b * nblk + s)
    const = lambda b, s: (0, 0)
    return pl.pallas_call(
        _mixer_kernel,
        grid=(batch, nblk),
        in_specs=[
            pl.BlockSpec((rows, BIG_COLS), tok),
            pl.BlockSpec((rows, LANES), tok),
            pl.BlockSpec((LANES, rows), tokt),
            pl.BlockSpec((rows, D_MODEL), tok),
        ] + [pl.BlockSpec(a.shape, const) for a in consts],
        out_specs=[
            pl.BlockSpec((rows, D_MODEL), tok),
            pl.BlockSpec((rows, D_MODEL), tok),
        ],
        out_shape=[
            jax.ShapeDtypeStruct((t, D_MODEL), F32),
            jax.ShapeDtypeStruct((t, D_MODEL), BF16),
        ],
        scratch_shapes=[
            pltpu.VMEM((rows + CONV_PAD, SSD_CONV_CH), F32),
            pltpu.VMEM((rows + CONV_PAD, GDN_CONV_CH), F32),
            pltpu.VMEM((SSD_GROUPS, SSD_STATE, SSD_GROUP_W), F32),
            pltpu.VMEM((GDN_HEADS, GDN_HEAD_K, GDN_HEAD_V), F32),
            pltpu.VMEM((rows, SSD_WIDTH), F32),
            pltpu.VMEM((rows, GDN_WIDTH), F32),
        ],
        compiler_params=pltpu.CompilerParams(
            dimension_semantics=("arbitrary", "arbitrary"), vmem_limit_bytes=VMEM_LIMIT),
        name="mixer",
    )(big, small, smallt, x2, *consts)


def _top_desc(s, count):
    neg_inf = jnp.float32(-jnp.inf)
    vals = []
    for _ in range(count):
        m = jnp.max(s, axis=0, keepdims=True)
        vals.append(m)
        s = jnp.where(s == m, neg_inf, s)
    return jnp.concatenate(vals, axis=0)


def _rank_products(a, b, sub8):
    cands = [b * a[0:1], b[0:8] * a[1:2]]
    for i in range(2, 8):
        cands.append(jnp.where(sub8 < (PEER_TOPK // (i + 1)), b[0:8] * a[i:i + 1], 0.0))
    cands.append(a[8:16] * b[0:1])
    return jnp.concatenate(cands, axis=0)


def _route_kernel(hn2_ref, wqt_ref, keys_ref, e1_ref, e2_ref, th_ref, qt_scr):
    tn = hn2_ref.shape[0]
    qt_scr[...] = lax.dot_general(wqt_ref[...], hn2_ref[...], NT_DIMS, preferred_element_type=F32)
    sub8 = lax.broadcasted_iota(jnp.int32, (SUBLANES, LANES), 0)

    def head_body(h, carry):
        q0 = pl.multiple_of(h * PEER_DK, PEER_DK)
        s1_all = _bdot(keys_ref[0], qt_scr[pl.ds(q0, PEER_HALF), :])
        s2_all = _bdot(keys_ref[1], qt_scr[pl.ds(q0 + PEER_HALF, PEER_HALF), :])
        th_rows = []
        for lg in range(tn // LANES):
            l0 = lg * LANES
            s1 = s1_all[:, l0:l0 + LANES]
            s2 = s2_all[:, l0:l0 + LANES]
            x1 = jnp.exp(s1 - jnp.max(s1, axis=0, keepdims=True))
            x2 = jnp.exp(s2 - jnp.max(s2, axis=0, keepdims=True))
            a = _top_desc(x1, PEER_TOPK)
            b = _top_desc(x2, PEER_TOPK)
            cand = _rank_products(a, b, sub8)
            theta = _top_desc(cand, PEER_TOPK)[PEER_TOPK - 1:PEER_TOPK]
            chosen = cand >= theta
            zsum = jnp.sum(jnp.where(chosen, cand, 0.0), axis=0, keepdims=True)
            rz = 1.0 / zsum
            cand_n = _rank_products(a * rz, b, sub8)
            th_rows.append(jnp.min(jnp.where(chosen, cand_n, jnp.float32(jnp.inf)), axis=0, keepdims=True))
            e1_ref[h, :, l0:l0 + LANES] = x1 * rz
            e2_ref[h, :, l0:l0 + LANES] = x2
        th_ref[pl.ds(h, 1), :] = jnp.concatenate(th_rows, axis=1)
        return carry

    lax.fori_loop(0, PEER_HEADS, head_body, 0)


def _route(hn2, wqt, keys, tn):
    t = hn2.shape[0]
    sc_spec = pl.BlockSpec((PEER_HEADS, N_KEYS, tn), lambda i: (0, 0, i))
    sc_shape = jax.ShapeDtypeStruct((PEER_HEADS, N_KEYS, t), F32)
    return pl.pallas_call(
        _route_kernel,
        grid=(t // tn,),
        in_specs=[
            pl.BlockSpec((tn, D_MODEL), lambda i: (i, 0)),
            pl.BlockSpec(wqt.shape, lambda i: (0, 0)),
            pl.BlockSpec(keys.shape, lambda i: (0, 0, 0)),
        ],
        out_specs=[sc_spec, sc_spec, pl.BlockSpec((PEER_HEADS, tn), lambda i: (0, i))],
        out_shape=[sc_shape, sc_shape, jax.ShapeDtypeStruct((PEER_HEADS, t), F32)],
        scratch_shapes=[pltpu.VMEM((PEER_HEADS * PEER_DK, tn), F32)],
        compiler_params=pltpu.CompilerParams(
            dimension_semantics=("arbitrary",), vmem_limit_bytes=VMEM_LIMIT),
        name="route",
    )(hn2, wqt, keys)


PEER_ROWS = 16
PEER_KCHUNK = 256


def _peer_kernel(hn2_ref, u_ref, vt_ref, e1_ref, e2_ref, th_ref, x1_ref, nw_ref,
                 out_ref, yt_scr, act_scr, w_scr, e1b_scr, thb_scr):
    j = pl.program_id(1)
    te = u_ref.shape[0]
    tm = hn2_ref.shape[0]
    n_e1 = te // N_KEYS

    @pl.when(j == 0)
    def _():
        yt_scr[...] = jnp.zeros(yt_scr.shape, F32)
        for h in range(PEER_HEADS):
            thb_scr[h] = jnp.broadcast_to(th_ref[h:h + 1, :], (SUBLANES, tm))

    hn2 = hn2_ref[...]
    for e1l in range(n_e1):
        act_scr[e1l * N_KEYS:(e1l + 1) * N_KEYS, :] = lax.dot_general(
            u_ref[e1l * N_KEYS:(e1l + 1) * N_KEYS, :], hn2, NT_DIMS, preferred_element_type=F32)

    y_acc = None
    for e1l in range(n_e1):
        e1 = j * n_e1 + e1l
        slot = e1l % 2
        for h in range(PEER_HEADS):
            e1b_scr[slot, h] = jnp.broadcast_to(e1_ref[h, pl.ds(e1, 1), :], (SUBLANES, tm))
        for r in range(N_KEYS // PEER_ROWS):
            r0 = r * PEER_ROWS
            acc = jnp.zeros((PEER_ROWS, tm), F32)
            for h in range(PEER_HEADS):
                rep = PEER_ROWS // SUBLANES
                e1b = jnp.concatenate([e1b_scr[slot, h]] * rep, axis=0)
                thb = jnp.concatenate([thb_scr[h]] * rep, axis=0)
                val = e2_ref[h, r0:r0 + PEER_ROWS, :] * e1b
                acc = acc + jnp.where(val >= thb, val, 0.0)
            a = act_scr[e1l * N_KEYS + r0:e1l * N_KEYS + r0 + PEER_ROWS, :]
            gelu = 0.5 * a * (1.0 + lax.erf(a * (2.0 ** -0.5)))
            w_scr[e1l * N_KEYS + r0:e1l * N_KEYS + r0 + PEER_ROWS, :] = (acc * gelu).astype(BF16)
        done = (e1l + 1) * N_KEYS
        if done % PEER_KCHUNK == 0:
            k0 = done - PEER_KCHUNK
            part = jnp.dot(vt_ref[:, k0:done], w_scr[k0:done, :], preferred_element_type=F32)
            y_acc = part if y_acc is None else y_acc + part
    yt_scr[...] += y_acc

    @pl.when(j == pl.num_programs(1) - 1)
    def _():
        x = x1_ref[...] + yt_scr[...].T
        out_ref[...] = x * lax.rsqrt(jnp.mean(x * x, axis=-1, keepdims=True) + EPS) * nw_ref[...]


def _peer(hn2, u_bf, vt_bf, e1, e2, th, x1, nw, tm, te):
    t = hn2.shape[0]
    sc_spec = pl.BlockSpec((PEER_HEADS, N_KEYS, tm), lambda i, j: (0, 0, i))
    return pl.pallas_call(
        _peer_kernel,
        grid=(t // tm, N_EXPERTS // te),
        in_specs=[
            pl.BlockSpec((tm, D_MODEL), lambda i, j: (i, 0)),
            pl.BlockSpec((te, D_MODEL), lambda i, j: (j, 0)),
            pl.BlockSpec((D_MODEL, te), lambda i, j: (0, j)),
            sc_spec, sc_spec,
            pl.BlockSpec((PEER_HEADS, tm), lambda i, j: (0, i)),
            pl.BlockSpec((tm, D_MODEL), lambda i, j: (i, 0)),
            pl.BlockSpec((1, D_MODEL), lambda i, j: (0, 0)),
        ],
        out_specs=pl.BlockSpec((tm, D_MODEL), lambda i, j: (i, 0)),
        out_shape=jax.ShapeDtypeStruct((t, D_MODEL), F32),
        scratch_shapes=[
            pltpu.VMEM((D_MODEL, tm), F32),
            pltpu.VMEM((te, tm), F32),
            pltpu.VMEM((te, tm), BF16),
            pltpu.VMEM((2, PEER_HEADS, SUBLANES, tm), F32),
            pltpu.VMEM((PEER_HEADS, SUBLANES, tm), F32),
        ],
        compiler_params=pltpu.CompilerParams(
            dimension_semantics=("arbitrary", "arbitrary"), vmem_limit_bytes=VMEM_LIMIT),
        name="peer",
    )(hn2, u_bf, vt_bf, e1, e2, th, x1, nw)


def _pick(n, pref):
    t = min(pref, n)
    while n % t:
        t //= 2
    return t


def _expansion(src_col0, n_heads, width):
    r = jnp.arange(LANES)[:, None]
    c = jnp.arange(n_heads * width)[None, :]
    return (r == src_col0 + c // width).astype(BF16)


def _chunk_masks(rows):
    r = jnp.arange(rows)[:, None]
    c = jnp.arange(rows)[None, :]
    same = (r // CHUNK) == (c // CHUNK)
    return ((same & (r >= c)).astype(BF16), (same & (r <= c)).astype(BF16), same.astype(BF16))


TILE_INPROJ = 512
TILE_MIXER = 256
TILE_ROUTE = 256
TILE_PEER_TOKENS = 512
TILE_PEER_EXPERTS = 1024


def kernel(x, norm_mix_w, w_in, ssd_conv_w, ssd_conv_b, ssd_dt_bias, ssd_a_log, ssd_d, ssd_norm_w,
           gdn_conv_w, gdn_dt_bias, gdn_a_log, gdn_norm_w, w_out, norm_ffn_w, peer_w_q, peer_sub_keys,
           peer_u, peer_v, norm_final_w):
    batch, seq, _ = x.shape
    assert w_in.shape[0] == 1, "single trunk layer"
    t = batch * seq
    x2 = x.reshape(t, D_MODEL)
    w_in = w_in[0]

    wbig = jnp.concatenate([w_in[:, OFF_SSD_Z:OFF_SSD_DT], w_in[:, OFF_GDN_QKV:OFF_GDN_BETA]],
                           axis=1).astype(BF16)
    wsm = jnp.concatenate([w_in[:, OFF_SSD_DT:OFF_GDN_QKV], w_in[:, OFF_GDN_BETA:IN_COLS]], axis=1)
    wsm = jnp.pad(wsm, ((0, 0), (0, LANES - wsm.shape[1]))).astype(BF16)
    zeros4 = jnp.zeros((GDN_HEADS,), F32)
    bias = jnp.concatenate([ssd_dt_bias[0].astype(F32), zeros4, gdn_dt_bias[0].astype(F32)])
    amul = jnp.concatenate([-jnp.exp(ssd_a_log[0].astype(F32)), zeros4, -jnp.exp(gdn_a_log[0].astype(F32))])
    pad = LANES - bias.shape[0]
    prow = jnp.pad(jnp.stack([bias, amul]), ((0, SUBLANES - 2), (0, pad)))
    pcol = jnp.pad(jnp.stack([bias, amul], axis=1), ((0, pad), (0, LANES - 2)))
    rows = _pick(seq, TILE_MIXER)
    tril, triu, blk = _chunk_masks(rows)
    consts = [
        ssd_conv_w[0].astype(F32), ssd_conv_b[0][None, :].astype(F32), gdn_conv_w[0].astype(F32),
        prow, pcol, tril, triu, blk,
        _expansion(SM_DT, SSD_HEADS, SSD_HEAD_DIM),
        _expansion(SM_BETA, GDN_HEADS, GDN_HEAD_V),
        _expansion(SM_ALPHA, GDN_HEADS, GDN_HEAD_V),
        jnp.repeat(ssd_d[0].astype(F32), SSD_HEAD_DIM)[None, :],
        ssd_norm_w[0][None, :].astype(F32), gdn_norm_w[0][None, :].astype(F32),
        w_out[0].astype(BF16), norm_ffn_w[0][None, :].astype(F32),
    ]

    big, small, smallt = _inproj(x2, norm_mix_w[0][None, :].astype(F32), wbig, wsm, wsm.T,
                                 _pick(t, TILE_INPROJ))
    x1, hn2 = _mixer(big, small, smallt, x2, consts, batch, seq, rows)
    e1, e2, th = _route(hn2, peer_w_q[0].T.astype(BF16), peer_sub_keys[0].astype(BF16),
                        _pick(t, TILE_ROUTE))
    out = _peer(hn2, peer_u[0].astype(BF16), peer_v[0].T.astype(BF16), e1, e2, th, x1,
                norm_final_w[None, :].astype(F32), _pick(t, TILE_PEER_TOKENS), TILE_PEER_EXPERTS)
    return out.reshape(batch, seq, D_MODEL)
```

```python
import functools
import math

import jax
import jax.numpy as jnp
from jax import lax
from jax.experimental import pallas as pl
from jax.experimental.pallas import tpu as pltpu

F32 = jnp.float32
BF16 = jnp.bfloat16

D_MODEL = 1024
CHUNK = 64
CONV_W = 4
EPS = 1e-6

SSD_HEADS = 8
SSD_HEAD_DIM = 64
SSD_WIDTH = SSD_HEADS * SSD_HEAD_DIM
SSD_GROUPS = 2
SSD_STATE = 128
SSD_BC = SSD_GROUPS * SSD_STATE
SSD_CONV_CH = SSD_WIDTH + 2 * SSD_BC
SSD_GROUP_W = SSD_WIDTH // SSD_GROUPS
SSD_GROUP_HEADS = SSD_HEADS // SSD_GROUPS

GDN_HEADS = 4
GDN_HEAD_K = 128
GDN_HEAD_V = 128
GDN_KEY_WIDTH = GDN_HEADS * GDN_HEAD_K
GDN_WIDTH = GDN_HEADS * GDN_HEAD_V
GDN_CONV_CH = 2 * GDN_KEY_WIDTH + GDN_WIDTH

MIX_WIDTH = SSD_WIDTH + GDN_WIDTH

OFF_SSD_Z = 0
OFF_SSD_XBC = OFF_SSD_Z + SSD_WIDTH
OFF_SSD_DT = OFF_SSD_XBC + SSD_CONV_CH
OFF_GDN_QKV = OFF_SSD_DT + SSD_HEADS
OFF_GDN_GATE = OFF_GDN_QKV + GDN_CONV_CH
OFF_GDN_BETA = OFF_GDN_GATE + GDN_WIDTH
OFF_GDN_ALPHA = OFF_GDN_BETA + GDN_HEADS
IN_COLS = OFF_GDN_ALPHA + GDN_HEADS

PEER_HEADS = 8
PEER_DK = 256
PEER_HALF = PEER_DK // 2
N_KEYS = 128
N_EXPERTS = N_KEYS * N_KEYS
PEER_TOPK = 16

LANES = 128
SUBLANES = 8
SM_DT = 0
SM_BETA = SM_DT + SSD_HEADS
SM_ALPHA = SM_BETA + GDN_HEADS
BIG_Z = 0
BIG_XBC = BIG_Z + SSD_WIDTH
BIG_QKV = BIG_XBC + SSD_CONV_CH
BIG_GATE = BIG_QKV + GDN_CONV_CH
BIG_COLS = BIG_GATE + GDN_WIDTH

CONV_PAD = 8
VMEM_LIMIT = 56 * 1024 * 1024

NT_DIMS = (((1,), (1,)), ((), ()))
TN_DIMS = (((0,), (0,)), ((), ()))


def _softplus(x):
    return jnp.maximum(x, 0.0) + jnp.log1p(jnp.exp(-jnp.abs(x)))


def _silu(x):
    return x * jax.nn.sigmoid(x)


def _bdot(a, b):
    return jnp.dot(a.astype(BF16), b.astype(BF16), preferred_element_type=F32)


def _bdot_nt(a, b):
    return lax.dot_general(a.astype(BF16), b.astype(BF16), NT_DIMS, preferred_element_type=F32)


def _bdot_tn(a, b):
    return lax.dot_general(a.astype(BF16), b.astype(BF16), TN_DIMS, preferred_element_type=F32)


def _split3(x):
    hi = x.astype(BF16)
    r = x - hi.astype(F32)
    mid = r.astype(BF16)
    lo = (r - mid.astype(F32)).astype(BF16)
    return hi, mid, lo


def _sel_dot(x, m01):
    hi, mid, lo = _split3(x)
    d = lambda a: jnp.dot(a, m01, preferred_element_type=F32)
    return d(hi) + d(mid) + d(lo)


def _sel_dot_l(m01, x):
    hi, mid, lo = _split3(x)
    d = lambda a: jnp.dot(m01, a, preferred_element_type=F32)
    return d(hi) + d(mid) + d(lo)


def _inproj_kernel(x_ref, nw_ref, wbig_ref, wsm_ref, wsmt_ref, big_ref, small_ref, smallt_ref):
    x = x_ref[...]
    h = x * lax.rsqrt(jnp.mean(x * x, axis=-1, keepdims=True) + EPS) * nw_ref[...]
    hb = h.astype(BF16)
    big_ref[...] = jnp.dot(hb, wbig_ref[...], preferred_element_type=F32)
    small_ref[...] = jnp.dot(hb, wsm_ref[...], preferred_element_type=F32)
    smallt_ref[...] = lax.dot_general(wsmt_ref[...], hb, NT_DIMS, preferred_element_type=F32)


def _inproj(x2, nw, wbig, wsm, wsmt, tm):
    t = x2.shape[0]
    const = lambda i: (0, 0)
    return pl.pallas_call(
        _inproj_kernel,
        grid=(t // tm,),
        in_specs=[
            pl.BlockSpec((tm, D_MODEL), lambda i: (i, 0)),
            pl.BlockSpec((1, D_MODEL), const),
            pl.BlockSpec((D_MODEL, BIG_COLS), const),
            pl.BlockSpec((D_MODEL, LANES), const),
            pl.BlockSpec((LANES, D_MODEL), const),
        ],
        out_specs=[
            pl.BlockSpec((tm, BIG_COLS), lambda i: (i, 0)),
            pl.BlockSpec((tm, LANES), lambda i: (i, 0)),
            pl.BlockSpec((LANES, tm), lambda i: (0, i)),
        ],
        out_shape=[
            jax.ShapeDtypeStruct((t, BIG_COLS), F32),
            jax.ShapeDtypeStruct((t, LANES), F32),
            jax.ShapeDtypeStruct((LANES, t), F32),
        ],
        compiler_params=pltpu.CompilerParams(
            dimension_semantics=("arbitrary",), vmem_limit_bytes=VMEM_LIMIT),
        name="inproj",
    )(x2, nw, wbig, wsm, wsmt)


def _causal_conv(ext_ref, u, w_ref, rows):
    ext_ref[CONV_PAD:CONV_PAD + rows, :] = u
    base = CONV_PAD - (CONV_W - 1)
    acc = w_ref[0:1, :] * ext_ref[base:base + rows, :]
    for j in range(1, CONV_W):
        acc = acc + w_ref[j:j + 1, :] * ext_ref[base + j:base + j + rows, :]
    ext_ref[base:CONV_PAD, :] = ext_ref[rows + base:rows + CONV_PAD, :]
    return acc


def _unit_lower_inverses(n_mats, eye):
    ps = [eye + n for n in n_mats]
    nks = list(n_mats)
    for _ in range(int(math.log2(CHUNK)) - 1):
        nks = [_bdot(nk, nk) for nk in nks]
        ps = [p + _bdot(p, nk) for p, nk in zip(ps, nks)]
    return ps


def _mixer_kernel(big_ref, small_ref, smallt_ref, x_ref,
                  scw_ref, scb_ref, gcw_ref, prow_ref, pcol_ref,
                  tril_ref, triu_ref, blk_ref, es_ref, eb_ref, eg_ref,
                  drow_ref, snw_ref, gnw_ref, wout_ref, nfw_ref,
                  x1_ref, hn2_ref,
                  ext_s, ext_g, sstate, gstate, ybuf, obuf):
    rows = x_ref.shape[0]
    n_chunks = rows // CHUNK

    @pl.when(pl.program_id(1) == 0)
    def _():
        ext_s[0:CONV_PAD, :] = jnp.zeros((CONV_PAD, SSD_CONV_CH), F32)
        ext_g[0:CONV_PAD, :] = jnp.zeros((CONV_PAD, GDN_CONV_CH), F32)
        sstate[...] = jnp.zeros(sstate.shape, F32)
        gstate[...] = jnp.zeros(gstate.shape, F32)

    sm = small_ref[...]
    sp = _softplus(sm + prow_ref[0:1, :])
    d_a = sp * prow_ref[1:2, :]
    beta = jax.nn.sigmoid(sm)
    smt = smallt_ref[...]
    spt = _softplus(smt + pcol_ref[:, 0:1])
    d_at = spt * pcol_ref[:, 1:2]

    cum = _sel_dot_l(tril_ref[...], d_a)
    tot = _sel_dot_l(blk_ref[...], d_a)
    cumt = _sel_dot(d_at, triu_ref[...])
    e_cum = jnp.exp(cum)
    e_end = jnp.exp(tot - cum)

    xw_scale = _sel_dot(sp * e_end, es_ref[...])
    yoff_scale = _sel_dot(e_cum, es_ref[...])
    beta_x = _sel_dot(beta, eb_ref[...])
    egc_x = _sel_dot(e_cum, eg_ref[...])
    ekd_x = _sel_dot(e_end, eg_ref[...])

    xbc = _silu(_causal_conv(ext_s, big_ref[:, BIG_XBC:BIG_QKV], scw_ref, rows) + scb_ref[...])
    qkv = _silu(_causal_conv(ext_g, big_ref[:, BIG_QKV:BIG_GATE], gcw_ref, rows))

    xs = xbc[:, :SSD_WIDTH]
    xw = xs * xw_scale

    def l2n(t):
        return t * lax.rsqrt(jnp.sum(t * t, axis=-1, keepdims=True) + EPS)

    q_parts, k_parts = [], []
    for hh in range(GDN_HEADS):
        q_parts.append(l2n(qkv[:, hh * GDN_HEAD_K:(hh + 1) * GDN_HEAD_K]))
        k_parts.append(l2n(qkv[:, GDN_KEY_WIDTH + hh * GDN_HEAD_K:GDN_KEY_WIDTH + (hh + 1) * GDN_HEAD_K]))
    q_all = jnp.concatenate(q_parts, axis=1) * (GDN_HEAD_K ** -0.5)
    k_all = jnp.concatenate(k_parts, axis=1)
    v_all = qkv[:, 2 * GDN_KEY_WIDTH:]
    kb = k_all * beta_x
    vb = v_all * beta_x
    kbg = kb * egc_x
    q_dec = q_all * egc_x
    k_dec = k_all * ekd_x

    li = lax.broadcasted_iota(jnp.int32, (CHUNK, CHUNK), 0)
    si = lax.broadcasted_iota(jnp.int32, (CHUNK, CHUNK), 1)
    incl = li >= si
    strict = li > si
    eye = jnp.where(li == si, 1.0, 0.0).astype(F32)
    neg_inf = jnp.float32(-jnp.inf)

    def decay(col, r0):
        seg = cum[r0:r0 + CHUNK, col:col + 1] - cumt[col:col + 1, r0:r0 + CHUNK]
        return jnp.exp(jnp.where(incl, seg, neg_inf))

    chunks = [(c, c * CHUNK) for c in range(n_chunks)]

    b_ms, c_ms = {}, {}
    for c, r0 in chunks:
        for g in range(SSD_GROUPS):
            b_ms[c, g] = xbc[r0:r0 + CHUNK, SSD_WIDTH + g * SSD_STATE:SSD_WIDTH + (g + 1) * SSD_STATE]
            c_ms[c, g] = xbc[r0:r0 + CHUNK,
                             SSD_WIDTH + SSD_BC + g * SSD_STATE:SSD_WIDTH + SSD_BC + (g + 1) * SSD_STATE]
    cbs = {k: _bdot_nt(c_ms[k], b_ms[k]) for k in b_ms}
    for c, r0 in chunks:
        for hh in range(SSD_HEADS):
            col = SM_DT + hh
            m_h = cbs[c, hh // SSD_GROUP_HEADS] * decay(col, r0) * spt[col:col + 1, r0:r0 + CHUNK]
            h0 = hh * SSD_HEAD_DIM
            ybuf[r0:r0 + CHUNK, h0:h0 + SSD_HEAD_DIM] = _bdot(m_h, xs[r0:r0 + CHUNK, h0:h0 + SSD_HEAD_DIM])

    decs, kqs = {}, {}
    for c, r0 in chunks:
        for hh in range(GDN_HEADS):
            k0 = hh * GDN_HEAD_K
            decs[c, hh] = decay(SM_ALPHA + hh, r0)
            lhs = jnp.concatenate([kb[r0:r0 + CHUNK, k0:k0 + GDN_HEAD_K],
                                   q_all[r0:r0 + CHUNK, k0:k0 + GDN_HEAD_K]], axis=0)
            kqs[c, hh] = _bdot_nt(lhs, k_all[r0:r0 + CHUNK, k0:k0 + GDN_HEAD_K])
    keys = list(decs)
    n_mats = [jnp.where(strict, -(kqs[k][:CHUNK] * decs[k]), 0.0) for k in keys]
    t_mats = dict(zip(keys, _unit_lower_inverses(n_mats, eye)))
    uws = {}
    for c, r0 in chunks:
        for hh in range(GDN_HEADS):
            k0 = hh * GDN_HEAD_K
            rhs = jnp.concatenate([vb[r0:r0 + CHUNK, k0:k0 + GDN_HEAD_K],
                                   kbg[r0:r0 + CHUNK, k0:k0 + GDN_HEAD_K]], axis=1)
            uws[c, hh] = _bdot(t_mats[c, hh], rhs)

    for c, r0 in chunks:
        r1 = r0 + CHUNK
        for g in range(SSD_GROUPS):
            c0 = g * SSD_GROUP_W
            st = sstate[g]
            y_off = _bdot(c_ms[c, g], st) * yoff_scale[r0:r1, c0:c0 + SSD_GROUP_W]
            ybuf[r0:r1, c0:c0 + SSD_GROUP_W] = (ybuf[r0:r1, c0:c0 + SSD_GROUP_W] + y_off
                                                + drow_ref[:, c0:c0 + SSD_GROUP_W] * xs[r0:r1, c0:c0 + SSD_GROUP_W])
            s_new = _bdot_tn(b_ms[c, g], xw[r0:r1, c0:c0 + SSD_GROUP_W])
            sstate[g] = st * yoff_scale[r1 - 1:r1, c0:c0 + SSD_GROUP_W] + s_new
        for hh in range(GDN_HEADS):
            k0 = hh * GDN_HEAD_K
            k1 = k0 + GDN_HEAD_K
            state = gstate[hh]
            lhs = jnp.concatenate([uws[c, hh][:, GDN_HEAD_V:], q_dec[r0:r1, k0:k1]], axis=0)
            ws_qs = _bdot(lhs, state)
            v_new = uws[c, hh][:, :GDN_HEAD_V] - ws_qs[:CHUNK]
            qk = kqs[c, hh][CHUNK:] * decs[c, hh]
            obuf[r0:r1, k0:k1] = ws_qs[CHUNK:] + _bdot(qk, v_new)
            gstate[hh] = state * egc_x[r1 - 1:r1, k0:k1] + _bdot_tn(k_dec[r0:r1, k0:k1], v_new)

    z = big_ref[:, BIG_Z:BIG_XBC]
    t = ybuf[...] * _silu(z)
    y_ssd = t * lax.rsqrt(jnp.mean(t * t, axis=-1, keepdims=True) + EPS) * snw_ref[...]
    o = obuf[...]
    gate = big_ref[:, BIG_GATE:BIG_COLS]
    parts = [y_ssd]
    for hh in range(GDN_HEADS):
        k0 = hh * GDN_HEAD_V
        o_h = o[:, k0:k0 + GDN_HEAD_V]
        parts.append(o_h * lax.rsqrt(jnp.mean(o_h * o_h, axis=-1, keepdims=True) + EPS)
                     * gnw_ref[...] * _silu(gate[:, k0:k0 + GDN_HEAD_V]))
    mixed = jnp.concatenate(parts, axis=1).astype(BF16)
    x1 = x_ref[...] + jnp.dot(mixed, wout_ref[...], preferred_element_type=F32)
    x1_ref[...] = x1
    hn2 = x1 * lax.rsqrt(jnp.mean(x1 * x1, axis=-1, keepdims=True) + EPS) * nfw_ref[...]
    hn2_ref[...] = hn2.astype(BF16)


def _mixer(big, small, smallt, x2, consts, batch, seq, rows):
    t = x2.shape[0]
    nblk = seq // rows
    tok = lambda b, s: (b * nblk + s, 0)
    tokt = lambda b, s: (0, b * nblk + s)
    const = lambda b, s: (0, 0)
    return pl.pallas_call(
        _mixer_kernel,
        grid=(batch, nblk),
        in_specs=[
            pl.BlockSpec((rows, BIG_COLS), tok),
            pl.BlockSpec((rows, LANES), tok),
            pl.BlockSpec((LANES, rows), tokt),
            pl.BlockSpec((rows, D_MODEL), tok),
        ] + [pl.BlockSpec(a.shape, const) for a in consts],
        out_specs=[
            pl.BlockSpec((rows, D_MODEL), tok),
            pl.BlockSpec((rows, D_MODEL), tok),
        ],
        out_shape=[
            jax.ShapeDtypeStruct((t, D_MODEL), F32),
            jax.ShapeDtypeStruct((t, D_MODEL), BF16),
        ],
        scratch_shapes=[
            pltpu.VMEM((rows + CONV_PAD, SSD_CONV_CH), F32),
            pltpu.VMEM((rows + CONV_PAD, GDN_CONV_CH), F32),
            pltpu.VMEM((SSD_GROUPS, SSD_STATE, SSD_GROUP_W), F32),
            pltpu.VMEM((GDN_HEADS, GDN_HEAD_K, GDN_HEAD_V), F32),
            pltpu.VMEM((rows, SSD_WIDTH), F32),
            pltpu.VMEM((rows, GDN_WIDTH), F32),
        ],
        compiler_params=pltpu.CompilerParams(
            dimension_semantics=("arbitrary", "arbitrary"), vmem_limit_bytes=VMEM_LIMIT),
        name="mixer",
    )(big, small, smallt, x2, *consts)


def _top_desc(s, count):
    neg_inf = jnp.float32(-jnp.inf)
    vals = []
    for _ in range(count):
        m = jnp.max(s, axis=0, keepdims=True)
        vals.append(m)
        s = jnp.where(s == m, neg_inf, s)
    return jnp.concatenate(vals, axis=0)


def _batcher_network(n):
    def merge(lo, hi, r):
        step = r * 2
        if step < hi - lo:
            yield from merge(lo, hi, step)
            yield from merge(lo + r, hi, step)
            yield from [(i, i + r) for i in range(lo + r, hi - r, step)]
        else:
            yield (lo, lo + r)

    def sort(lo, hi):
        if hi - lo >= 1:
            mid = lo + (hi - lo) // 2
            yield from sort(lo, mid)
            yield from sort(mid + 1, hi)
            yield from merge(lo, hi, 1)

    return list(sort(0, n - 1))


def _top_desc_tiles(x, count):
    n_tiles = x.shape[0] // SUBLANES
    v = [x[k * SUBLANES:(k + 1) * SUBLANES] for k in range(n_tiles)]
    for i, j in _batcher_network(n_tiles):
        v[i], v[j] = jnp.maximum(v[i], v[j]), jnp.minimum(v[i], v[j])
    vals = []
    for r in range(count):
        m = jnp.max(v[0], axis=0, keepdims=True)
        vals.append(m)
        hit = v[0] == m
        for k in range(count - r - 1):
            v[k] = jnp.where(hit, v[k + 1], v[k])
    return jnp.concatenate(vals, axis=0)


def _rank_products(a, b, sub8):
    cands = [b * a[0:1], b[0:8] * a[1:2]]
    for i in range(2, 8):
        cands.append(jnp.where(sub8 < (PEER_TOPK // (i + 1)), b[0:8] * a[i:i + 1], 0.0))
    cands.append(a[8:16] * b[0:1])
    return jnp.concatenate(cands, axis=0)


def _route_kernel(hn2_ref, wqt_ref, keys_ref, e1_ref, e2_ref, th_ref, qt_scr):
    tn = hn2_ref.shape[0]
    qt_scr[...] = lax.dot_general(wqt_ref[...], hn2_ref[...], NT_DIMS, preferred_element_type=F32)
    sub8 = lax.broadcasted_iota(jnp.int32, (SUBLANES, LANES), 0)

    def head_body(h, carry):
        q0 = pl.multiple_of(h * PEER_DK, PEER_DK)
        s1_all = _bdot(keys_ref[0], qt_scr[pl.ds(q0, PEER_HALF), :])
        s2_all = _bdot(keys_ref[1], qt_scr[pl.ds(q0 + PEER_HALF, PEER_HALF), :])
        th_rows = []
        for lg in range(tn // LANES):
            l0 = lg * LANES
            s1 = s1_all[:, l0:l0 + LANES]
            s2 = s2_all[:, l0:l0 + LANES]
            x1 = jnp.exp(s1 - jnp.max(s1, axis=0, keepdims=True))
            x2 = jnp.exp(s2 - jnp.max(s2, axis=0, keepdims=True))
            a = _top_desc_tiles(x1, PEER_TOPK)
            b = _top_desc_tiles(x2, PEER_TOPK)
            cand = _rank_products(a, b, sub8)
            theta = _top_desc(cand, PEER_TOPK)[PEER_TOPK - 1:PEER_TOPK]
            chosen = cand >= theta
            zsum = jnp.sum(jnp.where(chosen, cand, 0.0), axis=0, keepdims=True)
            rz = 0.5 / zsum
            cand_n = _rank_products(a * rz, b, sub8)
            th_rows.append(jnp.min(jnp.where(chosen, cand_n, jnp.float32(jnp.inf)), axis=0, keepdims=True))
            e1_ref[h, :, l0:l0 + LANES] = x1 * rz
            e2_ref[h, :, l0:l0 + LANES] = x2
        th_ref[pl.ds(h, 1), :] = jnp.concatenate(th_rows, axis=1)
        return carry

    lax.fori_loop(0, PEER_HEADS, head_body, 0)


def _route(hn2, wqt, keys, tn):
    t = hn2.shape[0]
    sc_spec = pl.BlockSpec((PEER_HEADS, N_KEYS, tn), lambda i: (0, 0, i))
    sc_shape = jax.ShapeDtypeStruct((PEER_HEADS, N_KEYS, t), F32)
    return pl.pallas_call(
        _route_kernel,
        grid=(t // tn,),
        in_specs=[
            pl.BlockSpec((tn, D_MODEL), lambda i: (i, 0)),
            pl.BlockSpec(wqt.shape, lambda i: (0, 0)),
            pl.BlockSpec(keys.shape, lambda i: (0, 0, 0)),
        ],
        out_specs=[sc_spec, sc_spec, pl.BlockSpec((PEER_HEADS, tn), lambda i: (0, i))],
        out_shape=[sc_shape, sc_shape, jax.ShapeDtypeStruct((PEER_HEADS, t), F32)],
        scratch_shapes=[pltpu.VMEM((PEER_HEADS * PEER_DK, tn), F32)],
        compiler_params=pltpu.CompilerParams(
            dimension_semantics=("arbitrary",), vmem_limit_bytes=VMEM_LIMIT),
        name="route",
    )(hn2, wqt, keys)


PEER_ROWS = 16


MXU_N = 256
PEER_PIECE_ROWS = 256
PEER_LANES = 256


def _peer_step(gate_tile, hn2_ref, u_ref, vt_ref, e1_ref, e2_ref, yt_scr, e1b_scr, thb_scr,
               act_write, act_read, w_write, w_read):
    te = u_ref.shape[0]
    tm = hn2_ref.shape[0]
    n_e1 = te // N_KEYS
    rep = PEER_ROWS // SUBLANES

    def first_matmul(m, n):
        rows = slice(m * PEER_PIECE_ROWS, (m + 1) * PEER_PIECE_ROWS)
        act_write[rows, n * MXU_N:(n + 1) * MXU_N] = lax.dot_general(
            u_ref[rows, :], hn2_ref[n * MXU_N:(n + 1) * MXU_N, :], NT_DIMS, preferred_element_type=F32)

    def second_matmul(m, n):
        rows = slice(m * PEER_PIECE_ROWS, (m + 1) * PEER_PIECE_ROWS)
        yt_scr[rows, n * MXU_N:(n + 1) * MXU_N] += jnp.dot(
            vt_ref[rows, :], w_read[:, n * MXU_N:(n + 1) * MXU_N], preferred_element_type=F32)

    for e1l in range(n_e1):
        for h in range(PEER_HEADS):
            e1b_scr[e1l, h] = jnp.broadcast_to(
                e1_ref[h, pl.ds(gate_tile * n_e1 + e1l, 1), :], (SUBLANES, tm))

    def gates(e1l, lb):
        n_groups = N_KEYS // PEER_ROWS
        lanes = slice(lb * PEER_LANES, (lb + 1) * PEER_LANES)
        accs = [None] * n_groups
        for h in range(PEER_HEADS):
            e1b = jnp.concatenate([e1b_scr[e1l, h, :, lanes]] * rep, axis=0)
            thb = jnp.concatenate([thb_scr[h, :, lanes]] * rep, axis=0)
            for r in range(n_groups):
                val = e2_ref[h, r * PEER_ROWS:(r + 1) * PEER_ROWS, lanes] * e1b
                sel = jnp.where(val >= thb, val, 0.0)
                accs[r] = sel if accs[r] is None else accs[r] + sel
        for r in range(n_groups):
            rows = slice(e1l * N_KEYS + r * PEER_ROWS, e1l * N_KEYS + (r + 1) * PEER_ROWS)
            a = act_read[rows, lanes]
            w_write[rows, lanes] = (accs[r] * (a * (1.0 + lax.erf(a * (2.0 ** -0.5))))).astype(BF16)

    pieces = ([functools.partial(first_matmul, m, n)
               for n in range(tm // MXU_N) for m in range(te // PEER_PIECE_ROWS)]
              + [functools.partial(second_matmul, m, n)
                 for n in range(tm // MXU_N) for m in range(D_MODEL // PEER_PIECE_ROWS)])
    blocks = [(e1l, lb) for e1l in range(n_e1) for lb in range(tm // PEER_LANES)]
    per_block = -(-len(pieces) // len(blocks))
    for i, (e1l, lb) in enumerate(blocks):
        for piece in pieces[i * per_block:(i + 1) * per_block]:
            piece()
        gates(e1l, lb)
    for piece in pieces[len(blocks) * per_block:]:
        piece()


def _peer_kernel(hn2_ref, u_ref, vt_ref, e1_ref, e2_ref, th_ref, x1_ref, nw_ref,
                 out_ref, yt_scr, act0_scr, act1_scr, w0_scr, w1_scr, e1b_scr, thb_scr, *, n_tiles, n_steps):
    s = pl.program_id(0)
    tm = hn2_ref.shape[0]
    gate_tile = jnp.clip(s - 1, 0, n_steps - 1) % n_tiles
    lag_tile = jnp.clip(s - 2, 0, n_steps - 1) % n_tiles

    @pl.when(s == 0)
    def _():
        act1_scr[...] = jnp.zeros(act1_scr.shape, F32)
        w1_scr[...] = jnp.zeros(w1_scr.shape, BF16)

    @pl.when(lag_tile == 0)
    def _():
        yt_scr[...] = jnp.zeros(yt_scr.shape, F32)

    @pl.when(gate_tile == 0)
    def _():
        for h in range(PEER_HEADS):
            thb_scr[h] = jnp.broadcast_to(th_ref[h:h + 1, :], (SUBLANES, tm))

    args = (gate_tile, hn2_ref, u_ref, vt_ref, e1_ref, e2_ref, yt_scr, e1b_scr, thb_scr)

    @pl.when(s % 2 == 0)
    def _():
        _peer_step(*args, act0_scr, act1_scr, w0_scr, w1_scr)

    @pl.when(s % 2 == 1)
    def _():
        _peer_step(*args, act1_scr, act0_scr, w1_scr, w0_scr)

    @pl.when((lag_tile == n_tiles - 1) & (s >= 2))
    def _():
        x = x1_ref[...] + yt_scr[...].T
        out_ref[...] = x * lax.rsqrt(jnp.mean(x * x, axis=-1, keepdims=True) + EPS) * nw_ref[...]


def _peer(hn2, u_bf, vt_bf, e1, e2, th, x1, nw, tm, te):
    t = hn2.shape[0]
    n_tiles = N_EXPERTS // te
    n_steps = (t // tm) * n_tiles
    assert tm % MXU_N == 0 and te % N_KEYS == 0
    tile_a = lambda s: jnp.minimum(s, n_steps - 1)
    tile_b = lambda s: jnp.clip(s - 1, 0, n_steps - 1)
    tile_c = lambda s: jnp.clip(s - 2, 0, n_steps - 1)
    sc_spec = pl.BlockSpec((PEER_HEADS, N_KEYS, tm), lambda s: (0, 0, tile_b(s) // n_tiles))
    return pl.pallas_call(
        functools.partial(_peer_kernel, n_tiles=n_tiles, n_steps=n_steps),
        grid=(n_steps + 2,),
        in_specs=[
            pl.BlockSpec((tm, D_MODEL), lambda s: (tile_a(s) // n_tiles, 0)),
            pl.BlockSpec((te, D_MODEL), lambda s: (tile_a(s) % n_tiles, 0)),
            pl.BlockSpec((D_MODEL, te), lambda s: (0, tile_c(s) % n_tiles)),
            sc_spec, sc_spec,
            pl.BlockSpec((PEER_HEADS, tm), lambda s: (0, tile_b(s) // n_tiles)),
            pl.BlockSpec((tm, D_MODEL), lambda s: (tile_c(s) // n_tiles, 0)),
            pl.BlockSpec((1, D_MODEL), lambda s: (0, 0)),
        ],
        out_specs=pl.BlockSpec((tm, D_MODEL), lambda s: (tile_c(s) // n_tiles, 0)),
        out_shape=jax.ShapeDtypeStruct((t, D_MODEL), F32),
        scratch_shapes=[
            pltpu.VMEM((D_MODEL, tm), F32),
            pltpu.VMEM((te, tm), F32),
            pltpu.VMEM((te, tm), F32),
            pltpu.VMEM((te, tm), BF16),
            pltpu.VMEM((te, tm), BF16),
            pltpu.VMEM((te // N_KEYS, PEER_HEADS, SUBLANES, tm), F32),
            pltpu.VMEM((PEER_HEADS, SUBLANES, tm), F32),
        ],
        compiler_params=pltpu.CompilerParams(
            dimension_semantics=("arbitrary",), vmem_limit_bytes=VMEM_LIMIT),
        name="peer",
    )(hn2, u_bf, vt_bf, e1, e2, th, x1, nw)


def _pick(n, pref):
    t = min(pref, n)
    while n % t:
        t //= 2
    return t


def _expansion(src_col0, n_heads, width):
    r = jnp.arange(LANES)[:, None]
    c = jnp.arange(n_heads * width)[None, :]
    return (r == src_col0 + c // width).astype(BF16)


def _chunk_masks(rows):
    r = jnp.arange(rows)[:, None]
    c = jnp.arange(rows)[None, :]
    same = (r // CHUNK) == (c // CHUNK)
    return ((same & (r >= c)).astype(BF16), (same & (r <= c)).astype(BF16), same.astype(BF16))


TILE_INPROJ = 512
TILE_MIXER = 256
TILE_ROUTE = 256
TILE_PEER_TOKENS = 512
TILE_PEER_EXPERTS = 1024


def kernel(x, norm_mix_w, w_in, ssd_conv_w, ssd_conv_b, ssd_dt_bias, ssd_a_log, ssd_d, ssd_norm_w,
           gdn_conv_w, gdn_dt_bias, gdn_a_log, gdn_norm_w, w_out, norm_ffn_w, peer_w_q, peer_sub_keys,
           peer_u, peer_v, norm_final_w):
    batch, seq, _ = x.shape
    assert w_in.shape[0] == 1, "single trunk layer"
    t = batch * seq
    x2 = x.reshape(t, D_MODEL)
    w_in = w_in[0]

    wbig = jnp.concatenate([w_in[:, OFF_SSD_Z:OFF_SSD_DT], w_in[:, OFF_GDN_QKV:OFF_GDN_BETA]],
                           axis=1).astype(BF16)
    wsm = jnp.concatenate([w_in[:, OFF_SSD_DT:OFF_GDN_QKV], w_in[:, OFF_GDN_BETA:IN_COLS]], axis=1)
    wsm = jnp.pad(wsm, ((0, 0), (0, LANES - wsm.shape[1]))).astype(BF16)
    zeros4 = jnp.zeros((GDN_HEADS,), F32)
    bias = jnp.concatenate([ssd_dt_bias[0].astype(F32), zeros4, gdn_dt_bias[0].astype(F32)])
    amul = jnp.concatenate([-jnp.exp(ssd_a_log[0].astype(F32)), zeros4, -jnp.exp(gdn_a_log[0].astype(F32))])
    pad = LANES - bias.shape[0]
    prow = jnp.pad(jnp.stack([bias, amul]), ((0, SUBLANES - 2), (0, pad)))
    pcol = jnp.pad(jnp.stack([bias, amul], axis=1), ((0, pad), (0, LANES - 2)))
    rows = _pick(seq, TILE_MIXER)
    tril, triu, blk = _chunk_masks(rows)
    consts = [
        ssd_conv_w[0].astype(F32), ssd_conv_b[0][None, :].astype(F32), gdn_conv_w[0].astype(F32),
        prow, pcol, tril, triu, blk,
        _expansion(SM_DT, SSD_HEADS, SSD_HEAD_DIM),
        _expansion(SM_BETA, GDN_HEADS, GDN_HEAD_V),
        _expansion(SM_ALPHA, GDN_HEADS, GDN_HEAD_V),
        jnp.repeat(ssd_d[0].astype(F32), SSD_HEAD_DIM)[None, :],
        ssd_norm_w[0][None, :].astype(F32), gdn_norm_w[0][None, :].astype(F32),
        w_out[0].astype(BF16), norm_ffn_w[0][None, :].astype(F32),
    ]

    big, small, smallt = _inproj(x2, norm_mix_w[0][None, :].astype(F32), wbig, wsm, wsm.T,
                                 _pick(t, TILE_INPROJ))
    x1, hn2 = _mixer(big, small, smallt, x2, consts, batch, seq, rows)
    e1, e2, th = _route(hn2, peer_w_q[0].T.astype(BF16), peer_sub_keys[0].astype(BF16),
                        _pick(t, TILE_ROUTE))
    out = _peer(hn2, peer_u[0].astype(BF16), peer_v[0].T.astype(BF16), e1, e2, th, x1,
                norm_final_w[None, :].astype(F32), _pick(t, TILE_PEER_TOKENS), TILE_PEER_EXPERTS)
    return out.reshape(batch, seq, D_MODEL)
```

```python
import functools
import math

import jax
import jax.numpy as jnp
from jax import lax
from jax.experimental import pallas as pl
from jax.experimental.pallas import tpu as pltpu

F32 = jnp.float32
BF16 = jnp.bfloat16

D_MODEL = 1024
CHUNK = 64
CONV_W = 4
EPS = 1e-6

SSD_HEADS = 8
SSD_HEAD_DIM = 64
SSD_WIDTH = SSD_HEADS * SSD_HEAD_DIM
SSD_GROUPS = 2
SSD_STATE = 128
SSD_BC = SSD_GROUPS * SSD_STATE
SSD_CONV_CH = SSD_WIDTH + 2 * SSD_BC
SSD_GROUP_W = SSD_WIDTH // SSD_GROUPS
SSD_GROUP_HEADS = SSD_HEADS // SSD_GROUPS

GDN_HEADS = 4
GDN_HEAD_K = 128
GDN_HEAD_V = 128
GDN_KEY_WIDTH = GDN_HEADS * GDN_HEAD_K
GDN_WIDTH = GDN_HEADS * GDN_HEAD_V
GDN_CONV_CH = 2 * GDN_KEY_WIDTH + GDN_WIDTH

MIX_WIDTH = SSD_WIDTH + GDN_WIDTH

OFF_SSD_Z = 0
OFF_SSD_XBC = OFF_SSD_Z + SSD_WIDTH
OFF_SSD_DT = OFF_SSD_XBC + SSD_CONV_CH
OFF_GDN_QKV = OFF_SSD_DT + SSD_HEADS
OFF_GDN_GATE = OFF_GDN_QKV + GDN_CONV_CH
OFF_GDN_BETA = OFF_GDN_GATE + GDN_WIDTH
OFF_GDN_ALPHA = OFF_GDN_BETA + GDN_HEADS
IN_COLS = OFF_GDN_ALPHA + GDN_HEADS

PEER_HEADS = 8
PEER_DK = 256
PEER_HALF = PEER_DK // 2
N_KEYS = 128
N_EXPERTS = N_KEYS * N_KEYS
PEER_TOPK = 16

LANES = 128
SUBLANES = 8
SM_DT = 0
SM_BETA = SM_DT + SSD_HEADS
SM_ALPHA = SM_BETA + GDN_HEADS
BIG_Z = 0
BIG_XBC = BIG_Z + SSD_WIDTH
BIG_QKV = BIG_XBC + SSD_CONV_CH
BIG_GATE = BIG_QKV + GDN_CONV_CH
BIG_COLS = BIG_GATE + GDN_WIDTH

CONV_PAD = 8
VMEM_LIMIT = 56 * 1024 * 1024

NT_DIMS = (((1,), (1,)), ((), ()))
TN_DIMS = (((0,), (0,)), ((), ()))


def _softplus(x):
    return jnp.maximum(x, 0.0) + jnp.log1p(jnp.exp(-jnp.abs(x)))


def _silu(x):
    return x * jax.nn.sigmoid(x)


def _bdot(a, b):
    return jnp.dot(a.astype(BF16), b.astype(BF16), preferred_element_type=F32)


def _bdot_nt(a, b):
    return lax.dot_general(a.astype(BF16), b.astype(BF16), NT_DIMS, preferred_element_type=F32)


def _bdot_tn(a, b):
    return lax.dot_general(a.astype(BF16), b.astype(BF16), TN_DIMS, preferred_element_type=F32)


def _split3(x):
    hi = x.astype(BF16)
    r = x - hi.astype(F32)
    mid = r.astype(BF16)
    lo = (r - mid.astype(F32)).astype(BF16)
    return hi, mid, lo


def _sel_dot(x, m01):
    hi, mid, lo = _split3(x)
    d = lambda a: jnp.dot(a, m01, preferred_element_type=F32)
    return d(hi) + d(mid) + d(lo)


def _sel_dot_l(m01, x):
    hi, mid, lo = _split3(x)
    d = lambda a: jnp.dot(m01, a, preferred_element_type=F32)
    return d(hi) + d(mid) + d(lo)


def _inproj_kernel(x_ref, nw_ref, wbig_ref, wsm_ref, wsmt_ref, big_ref, small_ref, smallt_ref):
    x = x_ref[...]
    h = x * lax.rsqrt(jnp.mean(x * x, axis=-1, keepdims=True) + EPS) * nw_ref[...]
    hb = h.astype(BF16)
    big_ref[...] = jnp.dot(hb, wbig_ref[...], preferred_element_type=F32)
    small_ref[...] = jnp.dot(hb, wsm_ref[...], preferred_element_type=F32)
    smallt_ref[...] = lax.dot_general(wsmt_ref[...], hb, NT_DIMS, preferred_element_type=F32)


def _inproj(x2, nw, wbig, wsm, wsmt, tm):
    t = x2.shape[0]
    const = lambda i: (0, 0)
    return pl.pallas_call(
        _inproj_kernel,
        grid=(t // tm,),
        in_specs=[
            pl.BlockSpec((tm, D_MODEL), lambda i: (i, 0)),
            pl.BlockSpec((1, D_MODEL), const),
            pl.BlockSpec((D_MODEL, BIG_COLS), const),
            pl.BlockSpec((D_MODEL, LANES), const),
            pl.BlockSpec((LANES, D_MODEL), const),
        ],
        out_specs=[
            pl.BlockSpec((tm, BIG_COLS), lambda i: (i, 0)),
            pl.BlockSpec((tm, LANES), lambda i: (i, 0)),
            pl.BlockSpec((LANES, tm), lambda i: (0, i)),
        ],
        out_shape=[
            jax.ShapeDtypeStruct((t, BIG_COLS), F32),
            jax.ShapeDtypeStruct((t, LANES), F32),
            jax.ShapeDtypeStruct((LANES, t), F32),
        ],
        compiler_params=pltpu.CompilerParams(
            dimension_semantics=("arbitrary",), vmem_limit_bytes=VMEM_LIMIT),
        name="inproj",
    )(x2, nw, wbig, wsm, wsmt)


def _causal_conv(ext_ref, u, w_ref, rows):
    ext_ref[CONV_PAD:CONV_PAD + rows, :] = u
    base = CONV_PAD - (CONV_W - 1)
    acc = w_ref[0:1, :] * ext_ref[base:base + rows, :]
    for j in range(1, CONV_W):
        acc = acc + w_ref[j:j + 1, :] * ext_ref[base + j:base + j + rows, :]
    ext_ref[base:CONV_PAD, :] = ext_ref[rows + base:rows + CONV_PAD, :]
    return acc


def _unit_lower_inverses(n_mats, eye):
    ps = [eye + n for n in n_mats]
    nks = list(n_mats)
    for _ in range(int(math.log2(CHUNK)) - 1):
        nks = [_bdot(nk, nk) for nk in nks]
        ps = [p + _bdot(p, nk) for p, nk in zip(ps, nks)]
    return ps


def _mixer_kernel(big_ref, small_ref, smallt_ref, x_ref,
                  scw_ref, scb_ref, gcw_ref, prow_ref, pcol_ref,
                  tril_ref, triu_ref, blk_ref, es_ref, eb_ref, eg_ref,
                  drow_ref, snw_ref, gnw_ref, wout_ref, nfw_ref,
                  x1_ref, hn2_ref,
                  ext_s, ext_g, sstate, gstate, ybuf, obuf):
    rows = x_ref.shape[0]
    n_chunks = rows // CHUNK

    @pl.when(pl.program_id(1) == 0)
    def _():
        ext_s[0:CONV_PAD, :] = jnp.zeros((CONV_PAD, SSD_CONV_CH), F32)
        ext_g[0:CONV_PAD, :] = jnp.zeros((CONV_PAD, GDN_CONV_CH), F32)
        sstate[...] = jnp.zeros(sstate.shape, F32)
        gstate[...] = jnp.zeros(gstate.shape, F32)

    sm = small_ref[...]
    sp = _softplus(sm + prow_ref[0:1, :])
    d_a = sp * prow_ref[1:2, :]
    beta = jax.nn.sigmoid(sm)
    smt = smallt_ref[...]
    spt = _softplus(smt + pcol_ref[:, 0:1])
    d_at = spt * pcol_ref[:, 1:2]

    cum = _sel_dot_l(tril_ref[...], d_a)
    tot = _sel_dot_l(blk_ref[...], d_a)
    cumt = _sel_dot(d_at, triu_ref[...])
    e_cum = jnp.exp(cum)
    e_end = jnp.exp(tot - cum)

    xw_scale = _sel_dot(sp * e_end, es_ref[...])
    yoff_scale = _sel_dot(e_cum, es_ref[...])
    beta_x = _sel_dot(beta, eb_ref[...])
    egc_x = _sel_dot(e_cum, eg_ref[...])
    ekd_x = _sel_dot(e_end, eg_ref[...])

    xbc = _silu(_causal_conv(ext_s, big_ref[:, BIG_XBC:BIG_QKV], scw_ref, rows) + scb_ref[...])
    qkv = _silu(_causal_conv(ext_g, big_ref[:, BIG_QKV:BIG_GATE], gcw_ref, rows))

    xs = xbc[:, :SSD_WIDTH]
    xw = xs * xw_scale

    def l2n(t):
        return t * lax.rsqrt(jnp.sum(t * t, axis=-1, keepdims=True) + EPS)

    q_parts, k_parts = [], []
    for hh in range(GDN_HEADS):
        q_parts.append(l2n(qkv[:, hh * GDN_HEAD_K:(hh + 1) * GDN_HEAD_K]))
        k_parts.append(l2n(qkv[:, GDN_KEY_WIDTH + hh * GDN_HEAD_K:GDN_KEY_WIDTH + (hh + 1) * GDN_HEAD_K]))
    q_all = jnp.concatenate(q_parts, axis=1) * (GDN_HEAD_K ** -0.5)
    k_all = jnp.concatenate(k_parts, axis=1)
    v_all = qkv[:, 2 * GDN_KEY_WIDTH:]
    kb = k_all * beta_x
    vb = v_all * beta_x
    kbg = kb * egc_x
    q_dec = q_all * egc_x
    k_dec = k_all * ekd_x

    li = lax.broadcasted_iota(jnp.int32, (CHUNK, CHUNK), 0)
    si = lax.broadcasted_iota(jnp.int32, (CHUNK, CHUNK), 1)
    incl = li >= si
    strict = li > si
    eye = jnp.where(li == si, 1.0, 0.0).astype(F32)
    neg_inf = jnp.float32(-jnp.inf)

    def decay(col, r0):
        seg = cum[r0:r0 + CHUNK, col:col + 1] - cumt[col:col + 1, r0:r0 + CHUNK]
        return jnp.exp(jnp.where(incl, seg, neg_inf))

    chunks = [(c, c * CHUNK) for c in range(n_chunks)]

    b_ms, c_ms = {}, {}
    for c, r0 in chunks:
        for g in range(SSD_GROUPS):
            b_ms[c, g] = xbc[r0:r0 + CHUNK, SSD_WIDTH + g * SSD_STATE:SSD_WIDTH + (g + 1) * SSD_STATE]
            c_ms[c, g] = xbc[r0:r0 + CHUNK,
                             SSD_WIDTH + SSD_BC + g * SSD_STATE:SSD_WIDTH + SSD_BC + (g + 1) * SSD_STATE]
    cbs = {k: _bdot_nt(c_ms[k], b_ms[k]) for k in b_ms}
    for c, r0 in chunks:
        for hh in range(SSD_HEADS):
            col = SM_DT + hh
            m_h = cbs[c, hh // SSD_GROUP_HEADS] * decay(col, r0) * spt[col:col + 1, r0:r0 + CHUNK]
            h0 = hh * SSD_HEAD_DIM
            ybuf[r0:r0 + CHUNK, h0:h0 + SSD_HEAD_DIM] = _bdot(m_h, xs[r0:r0 + CHUNK, h0:h0 + SSD_HEAD_DIM])

    decs, kqs = {}, {}
    for c, r0 in chunks:
        for hh in range(GDN_HEADS):
            k0 = hh * GDN_HEAD_K
            decs[c, hh] = decay(SM_ALPHA + hh, r0)
            lhs = jnp.concatenate([kb[r0:r0 + CHUNK, k0:k0 + GDN_HEAD_K],
                                   q_all[r0:r0 + CHUNK, k0:k0 + GDN_HEAD_K]], axis=0)
            kqs[c, hh] = _bdot_nt(lhs, k_all[r0:r0 + CHUNK, k0:k0 + GDN_HEAD_K])
    keys = list(decs)
    n_mats = [jnp.where(strict, -(kqs[k][:CHUNK] * decs[k]), 0.0) for k in keys]
    t_mats = dict(zip(keys, _unit_lower_inverses(n_mats, eye)))
    uws = {}
    for c, r0 in chunks:
        for hh in range(GDN_HEADS):
            k0 = hh * GDN_HEAD_K
            rhs = jnp.concatenate([vb[r0:r0 + CHUNK, k0:k0 + GDN_HEAD_K],
                                   kbg[r0:r0 + CHUNK, k0:k0 + GDN_HEAD_K]], axis=1)
            uws[c, hh] = _bdot(t_mats[c, hh], rhs)

    for c, r0 in chunks:
        r1 = r0 + CHUNK
        for g in range(SSD_GROUPS):
            c0 = g * SSD_GROUP_W
            st = sstate[g]
            y_off = _bdot(c_ms[c, g], st) * yoff_scale[r0:r1, c0:c0 + SSD_GROUP_W]
            ybuf[r0:r1, c0:c0 + SSD_GROUP_W] = (ybuf[r0:r1, c0:c0 + SSD_GROUP_W] + y_off
                                                + drow_ref[:, c0:c0 + SSD_GROUP_W] * xs[r0:r1, c0:c0 + SSD_GROUP_W])
            s_new = _bdot_tn(b_ms[c, g], xw[r0:r1, c0:c0 + SSD_GROUP_W])
            sstate[g] = st * yoff_scale[r1 - 1:r1, c0:c0 + SSD_GROUP_W] + s_new
        for hh in range(GDN_HEADS):
            k0 = hh * GDN_HEAD_K
            k1 = k0 + GDN_HEAD_K
            state = gstate[hh]
            lhs = jnp.concatenate([uws[c, hh][:, GDN_HEAD_V:], q_dec[r0:r1, k0:k1]], axis=0)
            ws_qs = _bdot(lhs, state)
            v_new = uws[c, hh][:, :GDN_HEAD_V] - ws_qs[:CHUNK]
            qk = kqs[c, hh][CHUNK:] * decs[c, hh]
            obuf[r0:r1, k0:k1] = ws_qs[CHUNK:] + _bdot(qk, v_new)
            gstate[hh] = state * egc_x[r1 - 1:r1, k0:k1] + _bdot_tn(k_dec[r0:r1, k0:k1], v_new)

    z = big_ref[:, BIG_Z:BIG_XBC]
    t = ybuf[...] * _silu(z)
    y_ssd = t * lax.rsqrt(jnp.mean(t * t, axis=-1, keepdims=True) + EPS) * snw_ref[...]
    o = obuf[...]
    gate = big_ref[:, BIG_GATE:BIG_COLS]
    parts = [y_ssd]
    for hh in range(GDN_HEADS):
        k0 = hh * GDN_HEAD_V
        o_h = o[:, k0:k0 + GDN_HEAD_V]
        parts.append(o_h * lax.rsqrt(jnp.mean(o_h * o_h, axis=-1, keepdims=True) + EPS)
                     * gnw_ref[...] * _silu(gate[:, k0:k0 + GDN_HEAD_V]))
    mixed = jnp.concatenate(parts, axis=1).astype(BF16)
    x1 = x_ref[...] + jnp.dot(mixed, wout_ref[...], preferred_element_type=F32)
    x1_ref[...] = x1
    hn2 = x1 * lax.rsqrt(jnp.mean(x1 * x1, axis=-1, keepdims=True) + EPS) * nfw_ref[...]
    hn2_ref[...] = hn2.astype(BF16)


def _mixer(big, small, smallt, x2, consts, batch, seq, rows):
    t = x2.shape[0]
    nblk = seq // rows
    tok = lambda b, s: (b * nblk + s, 0)
    tokt = lambda b, s: (0, b * nblk + s)
    const = lambda b, s: (0, 0)
    return pl.pallas_call(
        _mixer_kernel,
        grid=(batch, nblk),
        in_specs=[
            pl.BlockSpec((rows, BIG_COLS), tok),
            pl.BlockSpec((rows, LANES), tok),
            pl.BlockSpec((LANES, rows), tokt),
            pl.BlockSpec((rows, D_MODEL), tok),
        ] + [pl.BlockSpec(a.shape, const) for a in consts],
        out_specs=[
            pl.BlockSpec((rows, D_MODEL), tok),
            pl.BlockSpec((rows, D_MODEL), tok),
        ],
        out_shape=[
            jax.ShapeDtypeStruct((t, D_MODEL), F32),
            jax.ShapeDtypeStruct((t, D_MODEL), BF16),
        ],
        scratch_shapes=[
            pltpu.VMEM((rows + CONV_PAD, SSD_CONV_CH), F32),
            pltpu.VMEM((rows + CONV_PAD, GDN_CONV_CH), F32),
            pltpu.VMEM((SSD_GROUPS, SSD_STATE, SSD_GROUP_W), F32),
            pltpu.VMEM((GDN_HEADS, GDN_HEAD_K, GDN_HEAD_V), F32),
            pltpu.VMEM((rows, SSD_WIDTH), F32),
            pltpu.VMEM((rows, GDN_WIDTH), F32),
        ],
        compiler_params=pltpu.CompilerParams(
            dimension_semantics=("arbitrary", "arbitrary"), vmem_limit_bytes=VMEM_LIMIT),
        name="mixer",
    )(big, small, smallt, x2, *consts)


def _top_desc(s, count):
    neg_inf = jnp.float32(-jnp.inf)
    vals = []
    for _ in range(count):
        m = jnp.max(s, axis=0, keepdims=True)
        vals.append(m)
        s = jnp.where(s == m, neg_inf, s)
    return jnp.concatenate(vals, axis=0)


def _batcher_network(n):
    def merge(lo, hi, r):
        step = r * 2
        if step < hi - lo:
            yield from merge(lo, hi, step)
            yield from merge(lo + r, hi, step)
            yield from [(i, i + r) for i in range(lo + r, hi - r, step)]
        else:
            yield (lo, lo + r)

    def sort(lo, hi):
        if hi - lo >= 1:
            mid = lo + (hi - lo) // 2
            yield from sort(lo, mid)
            yield from sort(mid + 1, hi)
            yield from merge(lo, hi, 1)

    return list(sort(0, n - 1))


def _top_desc_tiles(x, count):
    n_tiles = x.shape[0] // SUBLANES
    v = [x[k * SUBLANES:(k + 1) * SUBLANES] for k in range(n_tiles)]
    for i, j in _batcher_network(n_tiles):
        v[i], v[j] = jnp.maximum(v[i], v[j]), jnp.minimum(v[i], v[j])
    vals = []
    for r in range(count):
        m = jnp.max(v[0], axis=0, keepdims=True)
        vals.append(m)
        hit = v[0] == m
        for k in range(count - r - 1):
            v[k] = jnp.where(hit, v[k + 1], v[k])
    return jnp.concatenate(vals, axis=0)


def _rank_products(a, b, sub8):
    cands = [b * a[0:1], b[0:8] * a[1:2]]
    for i in range(2, 8):
        cands.append(jnp.where(sub8 < (PEER_TOPK // (i + 1)), b[0:8] * a[i:i + 1], 0.0))
    cands.append(a[8:16] * b[0:1])
    return jnp.concatenate(cands, axis=0)


def _route_kernel(hn2_ref, wqt_ref, keys_ref, e1_ref, e2_ref, th_ref, qt_scr):
    tn = hn2_ref.shape[0]
    qt_scr[...] = lax.dot_general(wqt_ref[...], hn2_ref[...], NT_DIMS, preferred_element_type=F32)
    sub8 = lax.broadcasted_iota(jnp.int32, (SUBLANES, LANES), 0)

    def head_body(h, carry):
        q0 = pl.multiple_of(h * PEER_DK, PEER_DK)
        s1_all = _bdot(keys_ref[0], qt_scr[pl.ds(q0, PEER_HALF), :])
        s2_all = _bdot(keys_ref[1], qt_scr[pl.ds(q0 + PEER_HALF, PEER_HALF), :])
        th_rows = []
        for lg in range(tn // LANES):
            l0 = lg * LANES
            s1 = s1_all[:, l0:l0 + LANES]
            s2 = s2_all[:, l0:l0 + LANES]
            x1 = jnp.exp(s1 - jnp.max(s1, axis=0, keepdims=True))
            x2 = jnp.exp(s2 - jnp.max(s2, axis=0, keepdims=True))
            a = _top_desc_tiles(x1, PEER_TOPK)
            b = _top_desc_tiles(x2, PEER_TOPK)
            cand = _rank_products(a, b, sub8)
            theta = _top_desc(cand, PEER_TOPK)[PEER_TOPK - 1:PEER_TOPK]
            chosen = cand >= theta
            zsum = jnp.sum(jnp.where(chosen, cand, 0.0), axis=0, keepdims=True)
            rz = 0.5 / zsum
            cand_n = _rank_products(a * rz, b, sub8)
            th_rows.append(jnp.min(jnp.where(chosen, cand_n, jnp.float32(jnp.inf)), axis=0, keepdims=True))
            e1_ref[h, :, l0:l0 + LANES] = x1 * rz
            e2_ref[h, :, l0:l0 + LANES] = x2
        th_ref[pl.ds(h, 1), :] = jnp.concatenate(th_rows, axis=1)
        return carry

    lax.fori_loop(0, PEER_HEADS, head_body, 0)


def _route(hn2, wqt, keys, tn):
    t = hn2.shape[0]
    sc_spec = pl.BlockSpec((PEER_HEADS, N_KEYS, tn), lambda i: (0, 0, i))
    sc_shape = jax.ShapeDtypeStruct((PEER_HEADS, N_KEYS, t), F32)
    return pl.pallas_call(
        _route_kernel,
        grid=(t // tn,),
        in_specs=[
            pl.BlockSpec((tn, D_MODEL), lambda i: (i, 0)),
            pl.BlockSpec(wqt.shape, lambda i: (0, 0)),
            pl.BlockSpec(keys.shape, lambda i: (0, 0, 0)),
        ],
        out_specs=[sc_spec, sc_spec, pl.BlockSpec((PEER_HEADS, tn), lambda i: (0, i))],
        out_shape=[sc_shape, sc_shape, jax.ShapeDtypeStruct((PEER_HEADS, t), F32)],
        scratch_shapes=[pltpu.VMEM((PEER_HEADS * PEER_DK, tn), F32)],
        compiler_params=pltpu.CompilerParams(
            dimension_semantics=("arbitrary",), vmem_limit_bytes=VMEM_LIMIT),
        name="route",
    )(hn2, wqt, keys)


PEER_ROWS = 16


MXU_N = 256
PEER_PIECE_ROWS = 256
PEER_LANES = 256


def _peer_step(gate_tile, hn2_scr, u_ref, vt_ref, u_scr, vt_scr, e1_ref, e2_scr, yt_scr, e1b_scr, thb_scr,
               act_write, act_read, w_write, w_read):
    te = u_ref.shape[0]
    tm = hn2_scr.shape[0]
    n_e1 = te // N_KEYS
    rep = PEER_ROWS // SUBLANES

    u_scr[...] = u_ref[...]
    vt_scr[...] = vt_ref[...]

    def first_matmul(m, n):
        rows = slice(m * PEER_PIECE_ROWS, (m + 1) * PEER_PIECE_ROWS)
        act_write[rows, n * MXU_N:(n + 1) * MXU_N] = lax.dot_general(
            u_scr[rows, :], hn2_scr[n * MXU_N:(n + 1) * MXU_N, :], NT_DIMS, preferred_element_type=F32)

    def second_matmul(m, n):
        rows = slice(m * PEER_PIECE_ROWS, (m + 1) * PEER_PIECE_ROWS)
        yt_scr[rows, n * MXU_N:(n + 1) * MXU_N] += jnp.dot(
            vt_scr[rows, :], w_read[:, n * MXU_N:(n + 1) * MXU_N], preferred_element_type=F32)

    for e1l in range(n_e1):
        for h in range(PEER_HEADS):
            e1b_scr[e1l, h] = jnp.broadcast_to(
                e1_ref[h, pl.ds(gate_tile * n_e1 + e1l, 1), :], (SUBLANES, tm))

    def gates(e1l, lb):
        n_groups = N_KEYS // PEER_ROWS
        lanes = slice(lb * PEER_LANES, (lb + 1) * PEER_LANES)
        accs = [None] * n_groups
        for h in range(PEER_HEADS):
            e1b = jnp.concatenate([e1b_scr[e1l, h, :, lanes]] * rep, axis=0)
            thb = jnp.concatenate([thb_scr[h, :, lanes]] * rep, axis=0)
            for r in range(n_groups):
                val = e2_scr[h, r * PEER_ROWS:(r + 1) * PEER_ROWS, lanes] * e1b
                sel = jnp.where(val >= thb, val, 0.0)
                accs[r] = sel if accs[r] is None else accs[r] + sel
        for r in range(n_groups):
            rows = slice(e1l * N_KEYS + r * PEER_ROWS, e1l * N_KEYS + (r + 1) * PEER_ROWS)
            a = act_read[rows, lanes]
            w_write[rows, lanes] = (accs[r] * (a * (1.0 + lax.erf(a * (2.0 ** -0.5))))).astype(BF16)

    pieces = ([functools.partial(first_matmul, m, n)
               for n in range(tm // MXU_N) for m in range(te // PEER_PIECE_ROWS)]
              + [functools.partial(second_matmul, m, n)
                 for n in range(tm // MXU_N) for m in range(D_MODEL // PEER_PIECE_ROWS)])
    blocks = [(e1l, lb) for e1l in range(n_e1) for lb in range(tm // PEER_LANES)]
    per_block = -(-len(pieces) // len(blocks))
    for i, (e1l, lb) in enumerate(blocks):
        for piece in pieces[i * per_block:(i + 1) * per_block]:
            piece()
        gates(e1l, lb)
    for piece in pieces[len(blocks) * per_block:]:
        piece()


def _peer_kernel(hn2_ref, u_ref, vt_ref, e1_ref, e2_ref, th_ref, x1_ref, nw_ref,
                 out_ref, yt_scr, act0_scr, act1_scr, w0_scr, w1_scr, e1b_scr, thb_scr, e2_scr,
                 hn2_scr, u_scr, vt_scr,
                 *, n_tiles, n_steps):
    s = pl.program_id(0)
    tm = hn2_ref.shape[0]
    gate_tile = jnp.clip(s - 1, 0, n_steps - 1) % n_tiles
    lag_tile = jnp.clip(s - 2, 0, n_steps - 1) % n_tiles

    @pl.when(s == 0)
    def _():
        act1_scr[...] = jnp.zeros(act1_scr.shape, F32)
        w1_scr[...] = jnp.zeros(w1_scr.shape, BF16)

    @pl.when(lag_tile == 0)
    def _():
        yt_scr[...] = jnp.zeros(yt_scr.shape, F32)

    @pl.when(gate_tile == 0)
    def _():
        for h in range(PEER_HEADS):
            thb_scr[h] = jnp.broadcast_to(th_ref[h:h + 1, :], (SUBLANES, tm))
            e2_scr[h] = e2_ref[h]

    @pl.when(jnp.minimum(s, n_steps - 1) % n_tiles == 0)
    def _():
        hn2_scr[...] = hn2_ref[...]

    args = (gate_tile, hn2_scr, u_ref, vt_ref, u_scr, vt_scr, e1_ref, e2_scr, yt_scr, e1b_scr, thb_scr)

    @pl.when(s % 2 == 0)
    def _():
        _peer_step(*args, act0_scr, act1_scr, w0_scr, w1_scr)

    @pl.when(s % 2 == 1)
    def _():
        _peer_step(*args, act1_scr, act0_scr, w1_scr, w0_scr)

    @pl.when((lag_tile == n_tiles - 1) & (s >= 2))
    def _():
        x = x1_ref[...] + yt_scr[...].T
        out_ref[...] = x * lax.rsqrt(jnp.mean(x * x, axis=-1, keepdims=True) + EPS) * nw_ref[...]


def _peer(hn2, u_bf, vt_bf, e1, e2, th, x1, nw, tm, te):
    t = hn2.shape[0]
    n_tiles = N_EXPERTS // te
    n_steps = (t // tm) * n_tiles
    assert tm % MXU_N == 0 and te % N_KEYS == 0
    tile_a = lambda s: jnp.minimum(s, n_steps - 1)
    tile_b = lambda s: jnp.clip(s - 1, 0, n_steps - 1)
    tile_c = lambda s: jnp.clip(s - 2, 0, n_steps - 1)
    sc_spec = pl.BlockSpec((PEER_HEADS, N_KEYS, tm), lambda s: (0, 0, tile_b(s) // n_tiles))
    return pl.pallas_call(
        functools.partial(_peer_kernel, n_tiles=n_tiles, n_steps=n_steps),
        grid=(n_steps + 2,),
        in_specs=[
            pl.BlockSpec((tm, D_MODEL), lambda s: (tile_a(s) // n_tiles, 0)),
            pl.BlockSpec((te, D_MODEL), lambda s: (tile_a(s) % n_tiles, 0)),
            pl.BlockSpec((D_MODEL, te), lambda s: (0, tile_c(s) % n_tiles)),
            sc_spec, sc_spec,
            pl.BlockSpec((PEER_HEADS, tm), lambda s: (0, tile_b(s) // n_tiles)),
            pl.BlockSpec((tm, D_MODEL), lambda s: (tile_c(s) // n_tiles, 0)),
            pl.BlockSpec((1, D_MODEL), lambda s: (0, 0)),
        ],
        out_specs=pl.BlockSpec((tm, D_MODEL), lambda s: (tile_c(s) // n_tiles, 0)),
        out_shape=jax.ShapeDtypeStruct((t, D_MODEL), F32),
        scratch_shapes=[
            pltpu.VMEM((D_MODEL, tm), F32),
            pltpu.VMEM((te, tm), F32),
            pltpu.VMEM((te, tm), F32),
            pltpu.VMEM((te, tm), BF16),
            pltpu.VMEM((te, tm), BF16),
            pltpu.VMEM((te // N_KEYS, PEER_HEADS, SUBLANES, tm), F32),
            pltpu.VMEM((PEER_HEADS, SUBLANES, tm), F32),
            pltpu.VMEM((PEER_HEADS, N_KEYS, tm), F32),
            pltpu.VMEM((tm, D_MODEL), BF16),
            pltpu.VMEM((te, D_MODEL), BF16),
            pltpu.VMEM((D_MODEL, te), BF16),
        ],
        compiler_params=pltpu.CompilerParams(
            dimension_semantics=("arbitrary",), vmem_limit_bytes=VMEM_LIMIT),
        name="peer",
    )(hn2, u_bf, vt_bf, e1, e2, th, x1, nw)


def _pick(n, pref):
    t = min(pref, n)
    while n % t:
        t //= 2
    return t


def _expansion(src_col0, n_heads, width):
    r = jnp.arange(LANES)[:, None]
    c = jnp.arange(n_heads * width)[None, :]
    return (r == src_col0 + c // width).astype(BF16)


def _chunk_masks(rows):
    r = jnp.arange(rows)[:, None]
    c = jnp.arange(rows)[None, :]
    same = (r // CHUNK) == (c // CHUNK)
    return ((same & (r >= c)).astype(BF16), (same & (r <= c)).astype(BF16), same.astype(BF16))


TILE_INPROJ = 512
TILE_MIXER = 256
TILE_ROUTE = 512
TILE_PEER_TOKENS = 512
TILE_PEER_EXPERTS = 1024


def kernel(x, norm_mix_w, w_in, ssd_conv_w, ssd_conv_b, ssd_dt_bias, ssd_a_log, ssd_d, ssd_norm_w,
           gdn_conv_w, gdn_dt_bias, gdn_a_log, gdn_norm_w, w_out, norm_ffn_w, peer_w_q, peer_sub_keys,
           peer_u, peer_v, norm_final_w):
    batch, seq, _ = x.shape
    assert w_in.shape[0] == 1, "single trunk layer"
    t = batch * seq
    x2 = x.reshape(t, D_MODEL)
    w_in = w_in[0]

    wbig = jnp.concatenate([w_in[:, OFF_SSD_Z:OFF_SSD_DT], w_in[:, OFF_GDN_QKV:OFF_GDN_BETA]],
                           axis=1).astype(BF16)
    wsm = jnp.concatenate([w_in[:, OFF_SSD_DT:OFF_GDN_QKV], w_in[:, OFF_GDN_BETA:IN_COLS]], axis=1)
    wsm = jnp.pad(wsm, ((0, 0), (0, LANES - wsm.shape[1]))).astype(BF16)
    zeros4 = jnp.zeros((GDN_HEADS,), F32)
    bias = jnp.concatenate([ssd_dt_bias[0].astype(F32), zeros4, gdn_dt_bias[0].astype(F32)])
    amul = jnp.concatenate([-jnp.exp(ssd_a_log[0].astype(F32)), zeros4, -jnp.exp(gdn_a_log[0].astype(F32))])
    pad = LANES - bias.shape[0]
    prow = jnp.pad(jnp.stack([bias, amul]), ((0, SUBLANES - 2), (0, pad)))
    pcol = jnp.pad(jnp.stack([bias, amul], axis=1), ((0, pad), (0, LANES - 2)))
    rows = _pick(seq, TILE_MIXER)
    tril, triu, blk = _chunk_masks(rows)
    consts = [
        ssd_conv_w[0].astype(F32), ssd_conv_b[0][None, :].astype(F32), gdn_conv_w[0].astype(F32),
        prow, pcol, tril, triu, blk,
        _expansion(SM_DT, SSD_HEADS, SSD_HEAD_DIM),
        _expansion(SM_BETA, GDN_HEADS, GDN_HEAD_V),
        _expansion(SM_ALPHA, GDN_HEADS, GDN_HEAD_V),
        jnp.repeat(ssd_d[0].astype(F32), SSD_HEAD_DIM)[None, :],
        ssd_norm_w[0][None, :].astype(F32), gdn_norm_w[0][None, :].astype(F32),
        w_out[0].astype(BF16), norm_ffn_w[0][None, :].astype(F32),
    ]

    big, small, smallt = _inproj(x2, norm_mix_w[0][None, :].astype(F32), wbig, wsm, wsm.T,
                                 _pick(t, TILE_INPROJ))
    x1, hn2 = _mixer(big, small, smallt, x2, consts, batch, seq, rows)
    e1, e2, th = _route(hn2, peer_w_q[0].T.astype(BF16), peer_sub_keys[0].astype(BF16),
                        _pick(t, TILE_ROUTE))
    out = _peer(hn2, peer_u[0].astype(BF16), peer_v[0].T.astype(BF16), e1, e2, th, x1,
                norm_final_w[None, :].astype(F32), _pick(t, TILE_PEER_TOKENS), TILE_PEER_EXPERTS)
    return out.reshape(batch, seq, D_MODEL)
```

```python
import functools
import math

import jax
import jax.numpy as jnp
from jax import lax
from jax.experimental import pallas as pl
from jax.experimental.pallas import tpu as pltpu

F32 = jnp.float32
BF16 = jnp.bfloat16

D_MODEL = 1024
CHUNK = 64
CONV_W = 4
EPS = 1e-6

SSD_HEADS = 8
SSD_HEAD_DIM = 64
SSD_WIDTH = SSD_HEADS * SSD_HEAD_DIM
SSD_GROUPS = 2
SSD_STATE = 128
SSD_BC = SSD_GROUPS * SSD_STATE
SSD_CONV_CH = SSD_WIDTH + 2 * SSD_BC
SSD_GROUP_W = SSD_WIDTH // SSD_GROUPS
SSD_GROUP_HEADS = SSD_HEADS // SSD_GROUPS

GDN_HEADS = 4
GDN_HEAD_K = 128
GDN_HEAD_V = 128
GDN_KEY_WIDTH = GDN_HEADS * GDN_HEAD_K
GDN_WIDTH = GDN_HEADS * GDN_HEAD_V
GDN_CONV_CH = 2 * GDN_KEY_WIDTH + GDN_WIDTH

MIX_WIDTH = SSD_WIDTH + GDN_WIDTH

OFF_SSD_Z = 0
OFF_SSD_XBC = OFF_SSD_Z + SSD_WIDTH
OFF_SSD_DT = OFF_SSD_XBC + SSD_CONV_CH
OFF_GDN_QKV = OFF_SSD_DT + SSD_HEADS
OFF_GDN_GATE = OFF_GDN_QKV + GDN_CONV_CH
OFF_GDN_BETA = OFF_GDN_GATE + GDN_WIDTH
OFF_GDN_ALPHA = OFF_GDN_BETA + GDN_HEADS
IN_COLS = OFF_GDN_ALPHA + GDN_HEADS

PEER_HEADS = 8
PEER_DK = 256
PEER_HALF = PEER_DK // 2
N_KEYS = 128
N_EXPERTS = N_KEYS * N_KEYS
PEER_TOPK = 16

LANES = 128
SUBLANES = 8
SM_DT = 0
SM_BETA = SM_DT + SSD_HEADS
SM_ALPHA = SM_BETA + GDN_HEADS
BIG_Z = 0
BIG_XBC = BIG_Z + SSD_WIDTH
BIG_QKV = BIG_XBC + SSD_CONV_CH
BIG_GATE = BIG_QKV + GDN_CONV_CH
BIG_COLS = BIG_GATE + GDN_WIDTH

CONV_PAD = 8
VMEM_LIMIT = 56 * 1024 * 1024

NT_DIMS = (((1,), (1,)), ((), ()))
TN_DIMS = (((0,), (0,)), ((), ()))


def _softplus(x):
    return jnp.maximum(x, 0.0) + jnp.log1p(jnp.exp(-jnp.abs(x)))


def _silu(x):
    return x * jax.nn.sigmoid(x)


def _bdot(a, b):
    return jnp.dot(a.astype(BF16), b.astype(BF16), preferred_element_type=F32)


def _bdot_nt(a, b):
    return lax.dot_general(a.astype(BF16), b.astype(BF16), NT_DIMS, preferred_element_type=F32)


def _bdot_tn(a, b):
    return lax.dot_general(a.astype(BF16), b.astype(BF16), TN_DIMS, preferred_element_type=F32)


def _split3(x):
    hi = x.astype(BF16)
    r = x - hi.astype(F32)
    mid = r.astype(BF16)
    lo = (r - mid.astype(F32)).astype(BF16)
    return hi, mid, lo


def _sel_dot(x, m01):
    hi, mid, lo = _split3(x)
    d = lambda a: jnp.dot(a, m01, preferred_element_type=F32)
    return d(hi) + d(mid) + d(lo)


def _sel_dot_l(m01, x):
    hi, mid, lo = _split3(x)
    d = lambda a: jnp.dot(m01, a, preferred_element_type=F32)
    return d(hi) + d(mid) + d(lo)


def _inproj_kernel(x_ref, nw_ref, wbig_ref, wsm_ref, wsmt_ref, big_ref, small_ref, smallt_ref):
    x = x_ref[...]
    h = x * lax.rsqrt(jnp.mean(x * x, axis=-1, keepdims=True) + EPS) * nw_ref[...]
    hb = h.astype(BF16)
    big_ref[...] = jnp.dot(hb, wbig_ref[...], preferred_element_type=F32)
    small_ref[...] = jnp.dot(hb, wsm_ref[...], preferred_element_type=F32)
    smallt_ref[...] = lax.dot_general(wsmt_ref[...], hb, NT_DIMS, preferred_element_type=F32)


def _inproj(x2, nw, wbig, wsm, wsmt, tm):
    t = x2.shape[0]
    const = lambda i: (0, 0)
    return pl.pallas_call(
        _inproj_kernel,
        grid=(t // tm,),
        in_specs=[
            pl.BlockSpec((tm, D_MODEL), lambda i: (i, 0)),
            pl.BlockSpec((1, D_MODEL), const),
            pl.BlockSpec((D_MODEL, BIG_COLS), const),
            pl.BlockSpec((D_MODEL, LANES), const),
            pl.BlockSpec((LANES, D_MODEL), const),
        ],
        out_specs=[
            pl.BlockSpec((tm, BIG_COLS), lambda i: (i, 0)),
            pl.BlockSpec((tm, LANES), lambda i: (i, 0)),
            pl.BlockSpec((LANES, tm), lambda i: (0, i)),
        ],
        out_shape=[
            jax.ShapeDtypeStruct((t, BIG_COLS), F32),
            jax.ShapeDtypeStruct((t, LANES), F32),
            jax.ShapeDtypeStruct((LANES, t), F32),
        ],
        compiler_params=pltpu.CompilerParams(
            dimension_semantics=("arbitrary",), vmem_limit_bytes=VMEM_LIMIT),
        name="inproj",
    )(x2, nw, wbig, wsm, wsmt)


def _causal_conv(ext_ref, u, w_ref, rows):
    ext_ref[CONV_PAD:CONV_PAD + rows, :] = u
    base = CONV_PAD - (CONV_W - 1)
    acc = w_ref[0:1, :] * ext_ref[base:base + rows, :]
    for j in range(1, CONV_W):
        acc = acc + w_ref[j:j + 1, :] * ext_ref[base + j:base + j + rows, :]
    ext_ref[base:CONV_PAD, :] = ext_ref[rows + base:rows + CONV_PAD, :]
    return acc


def _unit_lower_inverses(n_mats, eye):
    ps = [eye + n for n in n_mats]
    nks = list(n_mats)
    for _ in range(int(math.log2(CHUNK)) - 1):
        nks = [_bdot(nk, nk) for nk in nks]
        ps = [p + _bdot(p, nk) for p, nk in zip(ps, nks)]
    return ps


def _mixer_kernel(big_ref, small_ref, smallt_ref, x_ref,
                  scw_ref, scb_ref, gcw_ref, prow_ref, pcol_ref,
                  tril_ref, triu_ref, blk_ref, es_ref, eb_ref, eg_ref,
                  drow_ref, snw_ref, gnw_ref, wout_ref, nfw_ref,
                  x1_ref, hn2_ref,
                  ext_s, ext_g, sstate, gstate, ybuf, obuf):
    rows = x_ref.shape[0]
    n_chunks = rows // CHUNK

    @pl.when(pl.program_id(1) == 0)
    def _():
        ext_s[0:CONV_PAD, :] = jnp.zeros((CONV_PAD, SSD_CONV_CH), F32)
        ext_g[0:CONV_PAD, :] = jnp.zeros((CONV_PAD, GDN_CONV_CH), F32)
        sstate[...] = jnp.zeros(sstate.shape, F32)
        gstate[...] = jnp.zeros(gstate.shape, F32)

    sm = small_ref[...]
    sp = _softplus(sm + prow_ref[0:1, :])
    d_a = sp * prow_ref[1:2, :]
    beta = jax.nn.sigmoid(sm)
    smt = smallt_ref[...]
    spt = _softplus(smt + pcol_ref[:, 0:1])
    d_at = spt * pcol_ref[:, 1:2]

    cum = _sel_dot_l(tril_ref[...], d_a)
    tot = _sel_dot_l(blk_ref[...], d_a)
    cumt = _sel_dot(d_at, triu_ref[...])
    e_cum = jnp.exp(cum)
    e_end = jnp.exp(tot - cum)

    xw_scale = _sel_dot(sp * e_end, es_ref[...])
    yoff_scale = _sel_dot(e_cum, es_ref[...])
    beta_x = _sel_dot(beta, eb_ref[...])
    egc_x = _sel_dot(e_cum, eg_ref[...])
    ekd_x = _sel_dot(e_end, eg_ref[...])

    xbc = _silu(_causal_conv(ext_s, big_ref[:, BIG_XBC:BIG_QKV], scw_ref, rows) + scb_ref[...])
    qkv = _silu(_causal_conv(ext_g, big_ref[:, BIG_QKV:BIG_GATE], gcw_ref, rows))

    xs = xbc[:, :SSD_WIDTH]
    xw = xs * xw_scale

    def l2n(t):
        return t * lax.rsqrt(jnp.sum(t * t, axis=-1, keepdims=True) + EPS)

    q_parts, k_parts = [], []
    for hh in range(GDN_HEADS):
        q_parts.append(l2n(qkv[:, hh * GDN_HEAD_K:(hh + 1) * GDN_HEAD_K]))
        k_parts.append(l2n(qkv[:, GDN_KEY_WIDTH + hh * GDN_HEAD_K:GDN_KEY_WIDTH + (hh + 1) * GDN_HEAD_K]))
    q_all = jnp.concatenate(q_parts, axis=1) * (GDN_HEAD_K ** -0.5)
    k_all = jnp.concatenate(k_parts, axis=1)
    v_all = qkv[:, 2 * GDN_KEY_WIDTH:]
    kb = k_all * beta_x
    vb = v_all * beta_x
    kbg = kb * egc_x
    q_dec = q_all * egc_x
    k_dec = k_all * ekd_x

    li = lax.broadcasted_iota(jnp.int32, (CHUNK, CHUNK), 0)
    si = lax.broadcasted_iota(jnp.int32, (CHUNK, CHUNK), 1)
    incl = li >= si
    strict = li > si
    eye = jnp.where(li == si, 1.0, 0.0).astype(F32)
    neg_inf = jnp.float32(-jnp.inf)

    def decay(col, r0):
        seg = cum[r0:r0 + CHUNK, col:col + 1] - cumt[col:col + 1, r0:r0 + CHUNK]
        return jnp.exp(jnp.where(incl, seg, neg_inf))

    chunks = [(c, c * CHUNK) for c in range(n_chunks)]

    b_ms, c_ms = {}, {}
    for c, r0 in chunks:
        for g in range(SSD_GROUPS):
            b_ms[c, g] = xbc[r0:r0 + CHUNK, SSD_WIDTH + g * SSD_STATE:SSD_WIDTH + (g + 1) * SSD_STATE]
            c_ms[c, g] = xbc[r0:r0 + CHUNK,
                             SSD_WIDTH + SSD_BC + g * SSD_STATE:SSD_WIDTH + SSD_BC + (g + 1) * SSD_STATE]
    cbs = {k: _bdot_nt(c_ms[k], b_ms[k]) for k in b_ms}
    for c, r0 in chunks:
        for hh in range(SSD_HEADS):
            col = SM_DT + hh
            m_h = cbs[c, hh // SSD_GROUP_HEADS] * decay(col, r0) * spt[col:col + 1, r0:r0 + CHUNK]
            h0 = hh * SSD_HEAD_DIM
            ybuf[r0:r0 + CHUNK, h0:h0 + SSD_HEAD_DIM] = _bdot(m_h, xs[r0:r0 + CHUNK, h0:h0 + SSD_HEAD_DIM])

    decs, kqs = {}, {}
    for c, r0 in chunks:
        for hh in range(GDN_HEADS):
            k0 = hh * GDN_HEAD_K
            decs[c, hh] = decay(SM_ALPHA + hh, r0)
            lhs = jnp.concatenate([kb[r0:r0 + CHUNK, k0:k0 + GDN_HEAD_K],
                                   q_all[r0:r0 + CHUNK, k0:k0 + GDN_HEAD_K]], axis=0)
            kqs[c, hh] = _bdot_nt(lhs, k_all[r0:r0 + CHUNK, k0:k0 + GDN_HEAD_K])
    keys = list(decs)
    n_mats = [jnp.where(strict, -(kqs[k][:CHUNK] * decs[k]), 0.0) for k in keys]
    t_mats = dict(zip(keys, _unit_lower_inverses(n_mats, eye)))
    uws = {}
    for c, r0 in chunks:
        for hh in range(GDN_HEADS):
            k0 = hh * GDN_HEAD_K
            rhs = jnp.concatenate([vb[r0:r0 + CHUNK, k0:k0 + GDN_HEAD_K],
                                   kbg[r0:r0 + CHUNK, k0:k0 + GDN_HEAD_K]], axis=1)
            uws[c, hh] = _bdot(t_mats[c, hh], rhs)

    for c, r0 in chunks:
        r1 = r0 + CHUNK
        for g in range(SSD_GROUPS):
            c0 = g * SSD_GROUP_W
            st = sstate[g]
            y_off = _bdot(c_ms[c, g], st) * yoff_scale[r0:r1, c0:c0 + SSD_GROUP_W]
            ybuf[r0:r1, c0:c0 + SSD_GROUP_W] = (ybuf[r0:r1, c0:c0 + SSD_GROUP_W] + y_off
                                                + drow_ref[:, c0:c0 + SSD_GROUP_W] * xs[r0:r1, c0:c0 + SSD_GROUP_W])
            s_new = _bdot_tn(b_ms[c, g], xw[r0:r1, c0:c0 + SSD_GROUP_W])
            sstate[g] = st * yoff_scale[r1 - 1:r1, c0:c0 + SSD_GROUP_W] + s_new
        for hh in range(GDN_HEADS):
            k0 = hh * GDN_HEAD_K
            k1 = k0 + GDN_HEAD_K
            state = gstate[hh]
            lhs = jnp.concatenate([uws[c, hh][:, GDN_HEAD_V:], q_dec[r0:r1, k0:k1]], axis=0)
            ws_qs = _bdot(lhs, state)
            v_new = uws[c, hh][:, :GDN_HEAD_V] - ws_qs[:CHUNK]
            qk = kqs[c, hh][CHUNK:] * decs[c, hh]
            obuf[r0:r1, k0:k1] = ws_qs[CHUNK:] + _bdot(qk, v_new)
            gstate[hh] = state * egc_x[r1 - 1:r1, k0:k1] + _bdot_tn(k_dec[r0:r1, k0:k1], v_new)

    z = big_ref[:, BIG_Z:BIG_XBC]
    t = ybuf[...] * _silu(z)
    y_ssd = t * lax.rsqrt(jnp.mean(t * t, axis=-1, keepdims=True) + EPS) * snw_ref[...]
    o = obuf[...]
    gate = big_ref[:, BIG_GATE:BIG_COLS]
    parts = [y_ssd]
    for hh in range(GDN_HEADS):
        k0 = hh * GDN_HEAD_V
        o_h = o[:, k0:k0 + GDN_HEAD_V]
        parts.append(o_h * lax.rsqrt(jnp.mean(o_h * o_h, axis=-1, keepdims=True) + EPS)
                     * gnw_ref[...] * _silu(gate[:, k0:k0 + GDN_HEAD_V]))
    mixed = jnp.concatenate(parts, axis=1).astype(BF16)
    x1 = x_ref[...] + jnp.dot(mixed, wout_ref[...], preferred_element_type=F32)
    x1_ref[...] = x1
    hn2 = x1 * lax.rsqrt(jnp.mean(x1 * x1, axis=-1, keepdims=True) + EPS) * nfw_ref[...]
    hn2_ref[...] = hn2.astype(BF16)


def _mixer(big, small, smallt, x2, consts, batch, seq, rows):
    t = x2.shape[0]
    nblk = seq // rows
    tok = lambda b, s: (b * nblk + s, 0)
    tokt = lambda b, s: (0, b * nblk + s)
    const = lambda b, s: (0, 0)
    return pl.pallas_call(
        _mixer_kernel,
        grid=(batch, nblk),
        in_specs=[
            pl.BlockSpec((rows, BIG_COLS), tok),
            pl.BlockSpec((rows, LANES), tok),
            pl.BlockSpec((LANES, rows), tokt),
            pl.BlockSpec((rows, D_MODEL), tok),
        ] + [pl.BlockSpec(a.shape, const) for a in consts],
        out_specs=[
            pl.BlockSpec((rows, D_MODEL), tok),
            pl.BlockSpec((rows, D_MODEL), tok),
        ],
        out_shape=[
            jax.ShapeDtypeStruct((t, D_MODEL), F32),
            jax.ShapeDtypeStruct((t, D_MODEL), BF16),
        ],
        scratch_shapes=[
            pltpu.VMEM((rows + CONV_PAD, SSD_CONV_CH), F32),
            pltpu.VMEM((rows + CONV_PAD, GDN_CONV_CH), F32),
            pltpu.VMEM((SSD_GROUPS, SSD_STATE, SSD_GROUP_W), F32),
            pltpu.VMEM((GDN_HEADS, GDN_HEAD_K, GDN_HEAD_V), F32),
            pltpu.VMEM((rows, SSD_WIDTH), F32),
            pltpu.VMEM((rows, GDN_WIDTH), F32),
        ],
        compiler_params=pltpu.CompilerParams(
            dimension_semantics=("arbitrary", "arbitrary"), vmem_limit_bytes=VMEM_LIMIT),
        name="mixer",
    )(big, small, smallt, x2, *consts)


def _top_desc(s, count):
    neg_inf = jnp.float32(-jnp.inf)
    vals = []
    for _ in range(count):
        m = jnp.max(s, axis=0, keepdims=True)
        vals.append(m)
        s = jnp.where(s == m, neg_inf, s)
    return jnp.concatenate(vals, axis=0)


def _batcher_network(n):
    def merge(lo, hi, r):
        step = r * 2
        if step < hi - lo:
            yield from merge(lo, hi, step)
            yield from merge(lo + r, hi, step)
            yield from [(i, i + r) for i in range(lo + r, hi - r, step)]
        else:
            yield (lo, lo + r)

    def sort(lo, hi):
        if hi - lo >= 1:
            mid = lo + (hi - lo) // 2
            yield from sort(lo, mid)
            yield from sort(mid + 1, hi)
            yield from merge(lo, hi, 1)

    return list(sort(0, n - 1))


def _top_desc_tiles(x, count):
    n_tiles = x.shape[0] // SUBLANES
    v = [x[k * SUBLANES:(k + 1) * SUBLANES] for k in range(n_tiles)]
    for i, j in _batcher_network(n_tiles):
        v[i], v[j] = jnp.maximum(v[i], v[j]), jnp.minimum(v[i], v[j])
    vals = []
    for r in range(count):
        m = jnp.max(v[0], axis=0, keepdims=True)
        vals.append(m)
        hit = v[0] == m
        for k in range(count - r - 1):
            v[k] = jnp.where(hit, v[k + 1], v[k])
    return jnp.concatenate(vals, axis=0)


def _rank_products(a, b, sub8):
    cands = [b * a[0:1], b[0:8] * a[1:2]]
    for i in range(2, 8):
        cands.append(jnp.where(sub8 < (PEER_TOPK // (i + 1)), b[0:8] * a[i:i + 1], 0.0))
    cands.append(a[8:16] * b[0:1])
    return jnp.concatenate(cands, axis=0)


def _route_kernel(hn2_ref, wqt_ref, keys_ref, e1_ref, e2_ref, th_ref, qt_scr):
    tn = hn2_ref.shape[0]
    qt_scr[...] = lax.dot_general(wqt_ref[...], hn2_ref[...], NT_DIMS, preferred_element_type=F32)
    sub8 = lax.broadcasted_iota(jnp.int32, (SUBLANES, LANES), 0)

    def head_body(h, carry):
        q0 = pl.multiple_of(h * PEER_DK, PEER_DK)
        s1_all = _bdot(keys_ref[0], qt_scr[pl.ds(q0, PEER_HALF), :])
        s2_all = _bdot(keys_ref[1], qt_scr[pl.ds(q0 + PEER_HALF, PEER_HALF), :])
        th_rows = []
        for lg in range(tn // LANES):
            l0 = lg * LANES
            s1 = s1_all[:, l0:l0 + LANES]
            s2 = s2_all[:, l0:l0 + LANES]
            x1 = jnp.exp(s1 - jnp.max(s1, axis=0, keepdims=True))
            x2 = jnp.exp(s2 - jnp.max(s2, axis=0, keepdims=True))
            a = _top_desc_tiles(x1, PEER_TOPK)
            b = _top_desc_tiles(x2, PEER_TOPK)
            cand = _rank_products(a, b, sub8)
            theta = _top_desc(cand, PEER_TOPK)[PEER_TOPK - 1:PEER_TOPK]
            chosen = cand >= theta
            zsum = jnp.sum(jnp.where(chosen, cand, 0.0), axis=0, keepdims=True)
            rz = 0.5 / zsum
            cand_n = _rank_products(a * rz, b, sub8)
            th_rows.append(jnp.min(jnp.where(chosen, cand_n, jnp.float32(jnp.inf)), axis=0, keepdims=True))
            e1_ref[h, :, l0:l0 + LANES] = x1 * rz
            e2_ref[h, :, l0:l0 + LANES] = x2
        th_ref[pl.ds(h, 1), :] = jnp.concatenate(th_rows, axis=1)
        return carry

    lax.fori_loop(0, PEER_HEADS, head_body, 0)


def _route(hn2, wqt, keys, tn):
    t = hn2.shape[0]
    sc_spec = pl.BlockSpec((PEER_HEADS, N_KEYS, tn), lambda i: (0, 0, i))
    sc_shape = jax.ShapeDtypeStruct((PEER_HEADS, N_KEYS, t), F32)
    return pl.pallas_call(
        _route_kernel,
        grid=(t // tn,),
        in_specs=[
            pl.BlockSpec((tn, D_MODEL), lambda i: (i, 0)),
            pl.BlockSpec(wqt.shape, lambda i: (0, 0)),
            pl.BlockSpec(keys.shape, lambda i: (0, 0, 0)),
        ],
        out_specs=[sc_spec, sc_spec, pl.BlockSpec((PEER_HEADS, tn), lambda i: (0, i))],
        out_shape=[sc_shape, sc_shape, jax.ShapeDtypeStruct((PEER_HEADS, t), F32)],
        scratch_shapes=[pltpu.VMEM((PEER_HEADS * PEER_DK, tn), F32)],
        compiler_params=pltpu.CompilerParams(
            dimension_semantics=("arbitrary",), vmem_limit_bytes=VMEM_LIMIT),
        name="route",
    )(hn2, wqt, keys)


PEER_ROWS = 16


MXU_N = 256
PEER_PIECE_ROWS = 256
PEER_LANES = 256


def _peer_step(gate_tile, hn2_scr, u_ref, vt_ref, u_scr, vt_scr, e1_ref, e2_scr, yt_scr, e1b_scr, thb_scr,
               act_write, act_read, w_write, w_read):
    te = u_ref.shape[0]
    tm = hn2_scr.shape[0]
    n_e1 = te // N_KEYS
    rep = PEER_ROWS // SUBLANES

    u_scr[...] = u_ref[...]
    vt_scr[...] = vt_ref[...]

    def first_matmul(m, n):
        rows = slice(m * PEER_PIECE_ROWS, (m + 1) * PEER_PIECE_ROWS)
        act_write[rows, n * MXU_N:(n + 1) * MXU_N] = lax.dot_general(
            u_scr[rows, :], hn2_scr[n * MXU_N:(n + 1) * MXU_N, :], NT_DIMS, preferred_element_type=F32)

    def second_matmul(m, n):
        rows = slice(m * PEER_PIECE_ROWS, (m + 1) * PEER_PIECE_ROWS)
        yt_scr[rows, n * MXU_N:(n + 1) * MXU_N] += jnp.dot(
            vt_scr[rows, :], w_read[:, n * MXU_N:(n + 1) * MXU_N], preferred_element_type=F32)

    for e1l in range(n_e1):
        for h in range(PEER_HEADS):
            e1b_scr[e1l, h] = jnp.broadcast_to(
                e1_ref[h, pl.ds(gate_tile * n_e1 + e1l, 1), :], (SUBLANES, tm))

    def gates(e1l, lb):
        n_groups = N_KEYS // PEER_ROWS
        lanes = slice(lb * PEER_LANES, (lb + 1) * PEER_LANES)
        accs = [None] * n_groups
        for h in range(PEER_HEADS):
            e1b = jnp.concatenate([e1b_scr[e1l, h, :, lanes]] * rep, axis=0)
            thb = jnp.concatenate([thb_scr[h, :, lanes]] * rep, axis=0)
            for r in range(n_groups):
                val = e2_scr[h, r * PEER_ROWS:(r + 1) * PEER_ROWS, lanes] * e1b
                sel = jnp.where(val >= thb, val, 0.0)
                accs[r] = sel if accs[r] is None else accs[r] + sel
        for r in range(n_groups):
            rows = slice(e1l * N_KEYS + r * PEER_ROWS, e1l * N_KEYS + (r + 1) * PEER_ROWS)
            a = act_read[rows, lanes]
            w_write[rows, lanes] = (accs[r] * (a * (1.0 + lax.erf(a * (2.0 ** -0.5))))).astype(BF16)

    pieces = ([functools.partial(first_matmul, m, n)
               for n in range(tm // MXU_N) for m in range(te // PEER_PIECE_ROWS)]
              + [functools.partial(second_matmul, m, n)
                 for n in range(tm // MXU_N) for m in range(D_MODEL // PEER_PIECE_ROWS)])
    blocks = [(e1l, lb) for e1l in range(n_e1) for lb in range(tm // PEER_LANES)]
    per_block = -(-len(pieces) // len(blocks))
    for i, (e1l, lb) in enumerate(blocks):
        for piece in pieces[i * per_block:(i + 1) * per_block]:
            piece()
        gates(e1l, lb)
    for piece in pieces[len(blocks) * per_block:]:
        piece()


def _peer_kernel(hn2_ref, u_ref, vt_ref, e1_ref, e2_ref, th_ref, x1_ref, nw_ref,
                 out_ref, yt_scr, act0_scr, act1_scr, w0_scr, w1_scr, e1b_scr, thb_scr, e2_scr,
                 hn2_scr, u_scr, vt_scr,
                 *, n_tiles, n_steps):
    s = pl.program_id(0)
    tm = hn2_ref.shape[0]
    gate_tile = jnp.clip(s - 1, 0, n_steps - 1) % n_tiles
    lag_tile = jnp.clip(s - 2, 0, n_steps - 1) % n_tiles

    @pl.when(s == 0)
    def _():
        act1_scr[...] = jnp.zeros(act1_scr.shape, F32)
        w1_scr[...] = jnp.zeros(w1_scr.shape, BF16)

    @pl.when(lag_tile == 0)
    def _():
        yt_scr[...] = jnp.zeros(yt_scr.shape, F32)

    @pl.when(gate_tile == 0)
    def _():
        for h in range(PEER_HEADS):
            thb_scr[h] = jnp.broadcast_to(th_ref[h:h + 1, :], (SUBLANES, tm))
            e2_scr[h] = e2_ref[h]

    @pl.when(jnp.minimum(s, n_steps - 1) % n_tiles == 0)
    def _():
        hn2_scr[...] = hn2_ref[...]

    args = (gate_tile, hn2_scr, u_ref, vt_ref, u_scr, vt_scr, e1_ref, e2_scr, yt_scr, e1b_scr, thb_scr)

    @pl.when(s % 2 == 0)
    def _():
        _peer_step(*args, act0_scr, act1_scr, w0_scr, w1_scr)

    @pl.when(s % 2 == 1)
    def _():
        _peer_step(*args, act1_scr, act0_scr, w1_scr, w0_scr)

    @pl.when((lag_tile == n_tiles - 1) & (s >= 2))
    def _():
        x = x1_ref[...] + yt_scr[...].T
        out_ref[...] = x * lax.rsqrt(jnp.mean(x * x, axis=-1, keepdims=True) + EPS) * nw_ref[...]


def _peer(hn2, u_bf, vt_bf, e1, e2, th, x1, nw, tm, te):
    t = hn2.shape[0]
    n_tiles = N_EXPERTS // te
    n_steps = (t // tm) * n_tiles
    assert tm % MXU_N == 0 and te % N_KEYS == 0
    tile_a = lambda s: jnp.minimum(s, n_steps - 1)
    tile_b = lambda s: jnp.clip(s - 1, 0, n_steps - 1)
    tile_c = lambda s: jnp.clip(s - 2, 0, n_steps - 1)
    sc_spec = pl.BlockSpec((PEER_HEADS, N_KEYS, tm), lambda s: (0, 0, tile_b(s) // n_tiles))
    return pl.pallas_call(
        functools.partial(_peer_kernel, n_tiles=n_tiles, n_steps=n_steps),
        grid=(n_steps + 2,),
        in_specs=[
            pl.BlockSpec((tm, D_MODEL), lambda s: (tile_a(s) // n_tiles, 0)),
            pl.BlockSpec((te, D_MODEL), lambda s: (tile_a(s) % n_tiles, 0)),
            pl.BlockSpec((None, D_MODEL, te), lambda s: (tile_c(s) % n_tiles, 0, 0)),
            sc_spec, sc_spec,
            pl.BlockSpec((PEER_HEADS, tm), lambda s: (0, tile_b(s) // n_tiles)),
            pl.BlockSpec((tm, D_MODEL), lambda s: (tile_c(s) // n_tiles, 0)),
            pl.BlockSpec((1, D_MODEL), lambda s: (0, 0)),
        ],
        out_specs=pl.BlockSpec((tm, D_MODEL), lambda s: (tile_c(s) // n_tiles, 0)),
        out_shape=jax.ShapeDtypeStruct((t, D_MODEL), F32),
        scratch_shapes=[
            pltpu.VMEM((D_MODEL, tm), F32),
            pltpu.VMEM((te, tm), F32),
            pltpu.VMEM((te, tm), F32),
            pltpu.VMEM((te, tm), BF16),
            pltpu.VMEM((te, tm), BF16),
            pltpu.VMEM((te // N_KEYS, PEER_HEADS, SUBLANES, tm), F32),
            pltpu.VMEM((PEER_HEADS, SUBLANES, tm), F32),
            pltpu.VMEM((PEER_HEADS, N_KEYS, tm), F32),
            pltpu.VMEM((tm, D_MODEL), BF16),
            pltpu.VMEM((te, D_MODEL), BF16),
            pltpu.VMEM((D_MODEL, te), BF16),
        ],
        compiler_params=pltpu.CompilerParams(
            dimension_semantics=("arbitrary",), vmem_limit_bytes=VMEM_LIMIT),
        name="peer",
    )(hn2, u_bf, vt_bf, e1, e2, th, x1, nw)


def _pick(n, pref):
    t = min(pref, n)
    while n % t:
        t //= 2
    return t


def _expansion(src_col0, n_heads, width):
    r = jnp.arange(LANES)[:, None]
    c = jnp.arange(n_heads * width)[None, :]
    return (r == src_col0 + c // width).astype(BF16)


def _chunk_masks(rows):
    r = jnp.arange(rows)[:, None]
    c = jnp.arange(rows)[None, :]
    same = (r // CHUNK) == (c // CHUNK)
    return ((same & (r >= c)).astype(BF16), (same & (r <= c)).astype(BF16), same.astype(BF16))


TILE_INPROJ = 512
TILE_MIXER = 256
TILE_ROUTE = 512
TILE_PEER_TOKENS = 512
TILE_PEER_EXPERTS = 1024


def kernel(x, norm_mix_w, w_in, ssd_conv_w, ssd_conv_b, ssd_dt_bias, ssd_a_log, ssd_d, ssd_norm_w,
           gdn_conv_w, gdn_dt_bias, gdn_a_log, gdn_norm_w, w_out, norm_ffn_w, peer_w_q, peer_sub_keys,
           peer_u, peer_v, norm_final_w):
    batch, seq, _ = x.shape
    assert w_in.shape[0] == 1, "single trunk layer"
    t = batch * seq
    x2 = x.reshape(t, D_MODEL)
    w_in = w_in[0]

    wbig = jnp.concatenate([w_in[:, OFF_SSD_Z:OFF_SSD_DT], w_in[:, OFF_GDN_QKV:OFF_GDN_BETA]],
                           axis=1).astype(BF16)
    wsm = jnp.concatenate([w_in[:, OFF_SSD_DT:OFF_GDN_QKV], w_in[:, OFF_GDN_BETA:IN_COLS]], axis=1)
    wsm = jnp.pad(wsm, ((0, 0), (0, LANES - wsm.shape[1]))).astype(BF16)
    zeros4 = jnp.zeros((GDN_HEADS,), F32)
    bias = jnp.concatenate([ssd_dt_bias[0].astype(F32), zeros4, gdn_dt_bias[0].astype(F32)])
    amul = jnp.concatenate([-jnp.exp(ssd_a_log[0].astype(F32)), zeros4, -jnp.exp(gdn_a_log[0].astype(F32))])
    pad = LANES - bias.shape[0]
    prow = jnp.pad(jnp.stack([bias, amul]), ((0, SUBLANES - 2), (0, pad)))
    pcol = jnp.pad(jnp.stack([bias, amul], axis=1), ((0, pad), (0, LANES - 2)))
    rows = _pick(seq, TILE_MIXER)
    tril, triu, blk = _chunk_masks(rows)
    consts = [
        ssd_conv_w[0].astype(F32), ssd_conv_b[0][None, :].astype(F32), gdn_conv_w[0].astype(F32),
        prow, pcol, tril, triu, blk,
        _expansion(SM_DT, SSD_HEADS, SSD_HEAD_DIM),
        _expansion(SM_BETA, GDN_HEADS, GDN_HEAD_V),
        _expansion(SM_ALPHA, GDN_HEADS, GDN_HEAD_V),
        jnp.repeat(ssd_d[0].astype(F32), SSD_HEAD_DIM)[None, :],
        ssd_norm_w[0][None, :].astype(F32), gdn_norm_w[0][None, :].astype(F32),
        w_out[0].astype(BF16), norm_ffn_w[0][None, :].astype(F32),
    ]

    big, small, smallt = _inproj(x2, norm_mix_w[0][None, :].astype(F32), wbig, wsm, wsm.T,
                                 _pick(t, TILE_INPROJ))
    x1, hn2 = _mixer(big, small, smallt, x2, consts, batch, seq, rows)
    e1, e2, th = _route(hn2, peer_w_q[0].T.astype(BF16), peer_sub_keys[0].astype(BF16),
                        _pick(t, TILE_ROUTE))
    vt_tiles = peer_v[0].astype(BF16).reshape(N_EXPERTS // TILE_PEER_EXPERTS, TILE_PEER_EXPERTS,
                                               D_MODEL).transpose(0, 2, 1)
    out = _peer(hn2, peer_u[0].astype(BF16), vt_tiles, e1, e2, th, x1,
                norm_final_w[None, :].astype(F32), _pick(t, TILE_PEER_TOKENS), TILE_PEER_EXPERTS)
    return out.reshape(batch, seq, D_MODEL)
```

```python
import functools
import math

import jax
import jax.numpy as jnp
from jax import lax
from jax.experimental import pallas as pl
from jax.experimental.pallas import tpu as pltpu

F32 = jnp.float32
BF16 = jnp.bfloat16

D_MODEL = 1024
CHUNK = 64
CONV_W = 4
EPS = 1e-6

SSD_HEADS = 8
SSD_HEAD_DIM = 64
SSD_WIDTH = SSD_HEADS * SSD_HEAD_DIM
SSD_GROUPS = 2
SSD_STATE = 128
SSD_BC = SSD_GROUPS * SSD_STATE
SSD_CONV_CH = SSD_WIDTH + 2 * SSD_BC
SSD_GROUP_W = SSD_WIDTH // SSD_GROUPS
SSD_GROUP_HEADS = SSD_HEADS // SSD_GROUPS

GDN_HEADS = 4
GDN_HEAD_K = 128
GDN_HEAD_V = 128
GDN_KEY_WIDTH = GDN_HEADS * GDN_HEAD_K
GDN_WIDTH = GDN_HEADS * GDN_HEAD_V
GDN_CONV_CH = 2 * GDN_KEY_WIDTH + GDN_WIDTH

MIX_WIDTH = SSD_WIDTH + GDN_WIDTH

OFF_SSD_Z = 0
OFF_SSD_XBC = OFF_SSD_Z + SSD_WIDTH
OFF_SSD_DT = OFF_SSD_XBC + SSD_CONV_CH
OFF_GDN_QKV = OFF_SSD_DT + SSD_HEADS
OFF_GDN_GATE = OFF_GDN_QKV + GDN_CONV_CH
OFF_GDN_BETA = OFF_GDN_GATE + GDN_WIDTH
OFF_GDN_ALPHA = OFF_GDN_BETA + GDN_HEADS
IN_COLS = OFF_GDN_ALPHA + GDN_HEADS

PEER_HEADS = 8
PEER_DK = 256
PEER_HALF = PEER_DK // 2
N_KEYS = 128
N_EXPERTS = N_KEYS * N_KEYS
PEER_TOPK = 16

LANES = 128
SUBLANES = 8
SM_DT = 0
SM_BETA = SM_DT + SSD_HEADS
SM_ALPHA = SM_BETA + GDN_HEADS
BIG_Z = 0
BIG_XBC = BIG_Z + SSD_WIDTH
BIG_QKV = BIG_XBC + SSD_CONV_CH
BIG_GATE = BIG_QKV + GDN_CONV_CH
BIG_COLS = BIG_GATE + GDN_WIDTH

CONV_PAD = 8
VMEM_LIMIT = 56 * 1024 * 1024

NT_DIMS = (((1,), (1,)), ((), ()))
TN_DIMS = (((0,), (0,)), ((), ()))


def _softplus(x):
    return jnp.maximum(x, 0.0) + jnp.log1p(jnp.exp(-jnp.abs(x)))


def _silu(x):
    return x * jax.nn.sigmoid(x)


def _bdot(a, b):
    return jnp.dot(a.astype(BF16), b.astype(BF16), preferred_element_type=F32)


def _bdot_nt(a, b):
    return lax.dot_general(a.astype(BF16), b.astype(BF16), NT_DIMS, preferred_element_type=F32)


def _bdot_tn(a, b):
    return lax.dot_general(a.astype(BF16), b.astype(BF16), TN_DIMS, preferred_element_type=F32)


def _split3(x):
    hi = x.astype(BF16)
    r = x - hi.astype(F32)
    mid = r.astype(BF16)
    lo = (r - mid.astype(F32)).astype(BF16)
    return hi, mid, lo


def _sel_dot(x, m01):
    hi, mid, lo = _split3(x)
    d = lambda a: jnp.dot(a, m01, preferred_element_type=F32)
    return d(hi) + d(mid) + d(lo)


def _sel_dot_l(m01, x):
    hi, mid, lo = _split3(x)
    d = lambda a: jnp.dot(m01, a, preferred_element_type=F32)
    return d(hi) + d(mid) + d(lo)


def _inproj_kernel(x_ref, nw_ref, wbig_ref, wsm_ref, wsmt_ref, big_ref, small_ref, smallt_ref):
    x = x_ref[...]
    h = x * lax.rsqrt(jnp.mean(x * x, axis=-1, keepdims=True) + EPS) * nw_ref[...]
    hb = h.astype(BF16)
    big_ref[...] = jnp.dot(hb, wbig_ref[...], preferred_element_type=F32)
    small_ref[...] = jnp.dot(hb, wsm_ref[...], preferred_element_type=F32)
    smallt_ref[...] = lax.dot_general(wsmt_ref[...], hb, NT_DIMS, preferred_element_type=F32)


def _inproj(x2, nw, wbig, wsm, wsmt, tm):
    t = x2.shape[0]
    const = lambda i: (0, 0)
    return pl.pallas_call(
        _inproj_kernel,
        grid=(t // tm,),
        in_specs=[
            pl.BlockSpec((tm, D_MODEL), lambda i: (i, 0)),
            pl.BlockSpec((1, D_MODEL), const),
            pl.BlockSpec((D_MODEL, BIG_COLS), const),
            pl.BlockSpec((D_MODEL, LANES), const),
            pl.BlockSpec((LANES, D_MODEL), const),
        ],
        out_specs=[
            pl.BlockSpec((tm, BIG_COLS), lambda i: (i, 0)),
            pl.BlockSpec((tm, LANES), lambda i: (i, 0)),
            pl.BlockSpec((LANES, tm), lambda i: (0, i)),
        ],
        out_shape=[
            jax.ShapeDtypeStruct((t, BIG_COLS), F32),
            jax.ShapeDtypeStruct((t, LANES), F32),
            jax.ShapeDtypeStruct((LANES, t), F32),
        ],
        compiler_params=pltpu.CompilerParams(
            dimension_semantics=("arbitrary",), vmem_limit_bytes=VMEM_LIMIT),
        name="inproj",
    )(x2, nw, wbig, wsm, wsmt)


def _causal_conv(ext_ref, u, w_ref, rows):
    ext_ref[CONV_PAD:CONV_PAD + rows, :] = u
    base = CONV_PAD - (CONV_W - 1)
    acc = w_ref[0:1, :] * ext_ref[base:base + rows, :]
    for j in range(1, CONV_W):
        acc = acc + w_ref[j:j + 1, :] * ext_ref[base + j:base + j + rows, :]
    ext_ref[base:CONV_PAD, :] = ext_ref[rows + base:rows + CONV_PAD, :]
    return acc


def _unit_lower_inverses(n_mats, eye):
    ps = [eye + n for n in n_mats]
    nks = list(n_mats)
    for _ in range(int(math.log2(CHUNK)) - 1):
        nks = [_bdot(nk, nk) for nk in nks]
        ps = [p + _bdot(p, nk) for p, nk in zip(ps, nks)]
    return ps


def _mixer_kernel(big_ref, small_ref, smallt_ref, x_ref,
                  scw_ref, scb_ref, gcw_ref, prow_ref, pcol_ref,
                  tril_ref, triu_ref, blk_ref, es_ref, eb_ref, eg_ref,
                  drow_ref, snw_ref, gnw_ref, wout_ref, nfw_ref,
                  x1_ref, hn2_ref,
                  ext_s, ext_g, sstate, gstate, ybuf, obuf):
    rows = x_ref.shape[0]
    n_chunks = rows // CHUNK

    @pl.when(pl.program_id(1) == 0)
    def _():
        ext_s[0:CONV_PAD, :] = jnp.zeros((CONV_PAD, SSD_CONV_CH), F32)
        ext_g[0:CONV_PAD, :] = jnp.zeros((CONV_PAD, GDN_CONV_CH), F32)
        sstate[...] = jnp.zeros(sstate.shape, F32)
        gstate[...] = jnp.zeros(gstate.shape, F32)

    sm = small_ref[...]
    sp = _softplus(sm + prow_ref[0:1, :])
    d_a = sp * prow_ref[1:2, :]
    beta = jax.nn.sigmoid(sm)
    smt = smallt_ref[...]
    spt = _softplus(smt + pcol_ref[:, 0:1])
    d_at = spt * pcol_ref[:, 1:2]

    cum = _sel_dot_l(tril_ref[...], d_a)
    tot = _sel_dot_l(blk_ref[...], d_a)
    cumt = _sel_dot(d_at, triu_ref[...])
    e_cum = jnp.exp(cum)
    e_end = jnp.exp(tot - cum)

    xw_scale = _sel_dot(sp * e_end, es_ref[...])
    yoff_scale = _sel_dot(e_cum, es_ref[...])
    beta_x = _sel_dot(beta, eb_ref[...])
    egc_x = _sel_dot(e_cum, eg_ref[...])
    ekd_x = _sel_dot(e_end, eg_ref[...])

    xbc = _silu(_causal_conv(ext_s, big_ref[:, BIG_XBC:BIG_QKV], scw_ref, rows) + scb_ref[...])
    qkv = _silu(_causal_conv(ext_g, big_ref[:, BIG_QKV:BIG_GATE], gcw_ref, rows))

    xs = xbc[:, :SSD_WIDTH]
    xw = xs * xw_scale

    def l2n(t):
        return t * lax.rsqrt(jnp.sum(t * t, axis=-1, keepdims=True) + EPS)

    q_parts, k_parts = [], []
    for hh in range(GDN_HEADS):
        q_parts.append(l2n(qkv[:, hh * GDN_HEAD_K:(hh + 1) * GDN_HEAD_K]))
        k_parts.append(l2n(qkv[:, GDN_KEY_WIDTH + hh * GDN_HEAD_K:GDN_KEY_WIDTH + (hh + 1) * GDN_HEAD_K]))
    q_all = jnp.concatenate(q_parts, axis=1) * (GDN_HEAD_K ** -0.5)
    k_all = jnp.concatenate(k_parts, axis=1)
    v_all = qkv[:, 2 * GDN_KEY_WIDTH:]
    kb = k_all * beta_x
    vb = v_all * beta_x
    kbg = kb * egc_x
    q_dec = q_all * egc_x
    k_dec = k_all * ekd_x

    li = lax.broadcasted_iota(jnp.int32, (CHUNK, CHUNK), 0)
    si = lax.broadcasted_iota(jnp.int32, (CHUNK, CHUNK), 1)
    incl = li >= si
    strict = li > si
    eye = jnp.where(li == si, 1.0, 0.0).astype(F32)
    neg_inf = jnp.float32(-jnp.inf)

    def decay(col, r0):
        seg = cum[r0:r0 + CHUNK, col:col + 1] - cumt[col:col + 1, r0:r0 + CHUNK]
        return jnp.exp(jnp.where(incl, seg, neg_inf))

    chunks = [(c, c * CHUNK) for c in range(n_chunks)]

    b_ms, c_ms = {}, {}
    for c, r0 in chunks:
        for g in range(SSD_GROUPS):
            b_ms[c, g] = xbc[r0:r0 + CHUNK, SSD_WIDTH + g * SSD_STATE:SSD_WIDTH + (g + 1) * SSD_STATE]
            c_ms[c, g] = xbc[r0:r0 + CHUNK,
                             SSD_WIDTH + SSD_BC + g * SSD_STATE:SSD_WIDTH + SSD_BC + (g + 1) * SSD_STATE]
    cbs = {k: _bdot_nt(c_ms[k], b_ms[k]) for k in b_ms}
    for c, r0 in chunks:
        for hh in range(SSD_HEADS):
            col = SM_DT + hh
            m_h = cbs[c, hh // SSD_GROUP_HEADS] * decay(col, r0) * spt[col:col + 1, r0:r0 + CHUNK]
            h0 = hh * SSD_HEAD_DIM
            ybuf[r0:r0 + CHUNK, h0:h0 + SSD_HEAD_DIM] = _bdot(m_h, xs[r0:r0 + CHUNK, h0:h0 + SSD_HEAD_DIM])

    decs, kqs = {}, {}
    for c, r0 in chunks:
        for hh in range(GDN_HEADS):
            k0 = hh * GDN_HEAD_K
            decs[c, hh] = decay(SM_ALPHA + hh, r0)
            lhs = jnp.concatenate([kb[r0:r0 + CHUNK, k0:k0 + GDN_HEAD_K],
                                   q_all[r0:r0 + CHUNK, k0:k0 + GDN_HEAD_K]], axis=0)
            kqs[c, hh] = _bdot_nt(lhs, k_all[r0:r0 + CHUNK, k0:k0 + GDN_HEAD_K])
    keys = list(decs)
    n_mats = [jnp.where(strict, -(kqs[k][:CHUNK] * decs[k]), 0.0) for k in keys]
    t_mats = dict(zip(keys, _unit_lower_inverses(n_mats, eye)))
    uws = {}
    for c, r0 in chunks:
        for hh in range(GDN_HEADS):
            k0 = hh * GDN_HEAD_K
            rhs = jnp.concatenate([vb[r0:r0 + CHUNK, k0:k0 + GDN_HEAD_K],
                                   kbg[r0:r0 + CHUNK, k0:k0 + GDN_HEAD_K]], axis=1)
            uws[c, hh] = _bdot(t_mats[c, hh], rhs)

    for c, r0 in chunks:
        r1 = r0 + CHUNK
        for g in range(SSD_GROUPS):
            c0 = g * SSD_GROUP_W
            st = sstate[g]
            y_off = _bdot(c_ms[c, g], st) * yoff_scale[r0:r1, c0:c0 + SSD_GROUP_W]
            ybuf[r0:r1, c0:c0 + SSD_GROUP_W] = (ybuf[r0:r1, c0:c0 + SSD_GROUP_W] + y_off
                                                + drow_ref[:, c0:c0 + SSD_GROUP_W] * xs[r0:r1, c0:c0 + SSD_GROUP_W])
            s_new = _bdot_tn(b_ms[c, g], xw[r0:r1, c0:c0 + SSD_GROUP_W])
            sstate[g] = st * yoff_scale[r1 - 1:r1, c0:c0 + SSD_GROUP_W] + s_new
        for hh in range(GDN_HEADS):
            k0 = hh * GDN_HEAD_K
            k1 = k0 + GDN_HEAD_K
            state = gstate[hh]
            lhs = jnp.concatenate([uws[c, hh][:, GDN_HEAD_V:], q_dec[r0:r1, k0:k1]], axis=0)
            ws_qs = _bdot(lhs, state)
            v_new = uws[c, hh][:, :GDN_HEAD_V] - ws_qs[:CHUNK]
            qk = kqs[c, hh][CHUNK:] * decs[c, hh]
            obuf[r0:r1, k0:k1] = ws_qs[CHUNK:] + _bdot(qk, v_new)
            gstate[hh] = state * egc_x[r1 - 1:r1, k0:k1] + _bdot_tn(k_dec[r0:r1, k0:k1], v_new)

    z = big_ref[:, BIG_Z:BIG_XBC]
    t = ybuf[...] * _silu(z)
    y_ssd = t * lax.rsqrt(jnp.mean(t * t, axis=-1, keepdims=True) + EPS) * snw_ref[...]
    o = obuf[...]
    gate = big_ref[:, BIG_GATE:BIG_COLS]
    parts = [y_ssd]
    for hh in range(GDN_HEADS):
        k0 = hh * GDN_HEAD_V
        o_h = o[:, k0:k0 + GDN_HEAD_V]
        parts.append(o_h * lax.rsqrt(jnp.mean(o_h * o_h, axis=-1, keepdims=True) + EPS)
                     * gnw_ref[...] * _silu(gate[:, k0:k0 + GDN_HEAD_V]))
    mixed = jnp.concatenate(parts, axis=1).astype(BF16)
    x1 = x_ref[...] + jnp.dot(mixed, wout_ref[...], preferred_element_type=F32)
    x1_ref[...] = x1
    hn2 = x1 * lax.rsqrt(jnp.mean(x1 * x1, axis=-1, keepdims=True) + EPS) * nfw_ref[...]
    hn2_ref[...] = hn2.astype(BF16)


def _mixer(big, small, smallt, x2, consts, batch, seq, rows):
    t = x2.shape[0]
    nblk = seq // rows
    tok = lambda b, s: (b * nblk + s, 0)
    tokt = lambda b, s: (0, b * nblk + s)
    const = lambda b, s: (0, 0)
    return pl.pallas_call(
        _mixer_kernel,
        grid=(batch, nblk),
        in_specs=[
            pl.BlockSpec((rows, BIG_COLS), tok),
            pl.BlockSpec((rows, LANES), tok),
            pl.BlockSpec((LANES, rows), tokt),
            pl.BlockSpec((rows, D_MODEL), tok),
        ] + [pl.BlockSpec(a.shape, const) for a in consts],
        out_specs=[
            pl.BlockSpec((rows, D_MODEL), tok),
            pl.BlockSpec((rows, D_MODEL), tok),
        ],
        out_shape=[
            jax.ShapeDtypeStruct((t, D_MODEL), F32),
            jax.ShapeDtypeStruct((t, D_MODEL), BF16),
        ],
        scratch_shapes=[
            pltpu.VMEM((rows + CONV_PAD, SSD_CONV_CH), F32),
            pltpu.VMEM((rows + CONV_PAD, GDN_CONV_CH), F32),
            pltpu.VMEM((SSD_GROUPS, SSD_STATE, SSD_GROUP_W), F32),
            pltpu.VMEM((GDN_HEADS, GDN_HEAD_K, GDN_HEAD_V), F32),
            pltpu.VMEM((rows, SSD_WIDTH), F32),
            pltpu.VMEM((rows, GDN_WIDTH), F32),
        ],
        compiler_params=pltpu.CompilerParams(
            dimension_semantics=("arbitrary", "arbitrary"), vmem_limit_bytes=VMEM_LIMIT),
        name="mixer",
    )(big, small, smallt, x2, *consts)


def _top_desc(s, count):
    neg_inf = jnp.float32(-jnp.inf)
    vals = []
    for _ in range(count):
        m = jnp.max(s, axis=0, keepdims=True)
        vals.append(m)
        s = jnp.where(s == m, neg_inf, s)
    return jnp.concatenate(vals, axis=0)


def _batcher_network(n):
    def merge(lo, hi, r):
        step = r * 2
        if step < hi - lo:
            yield from merge(lo, hi, step)
            yield from merge(lo + r, hi, step)
            yield from [(i, i + r) for i in range(lo + r, hi - r, step)]
        else:
            yield (lo, lo + r)

    def sort(lo, hi):
        if hi - lo >= 1:
            mid = lo + (hi - lo) // 2
            yield from sort(lo, mid)
            yield from sort(mid + 1, hi)
            yield from merge(lo, hi, 1)

    return list(sort(0, n - 1))


def _top_desc_tiles(x, count):
    n_tiles = x.shape[0] // SUBLANES
    v = [x[k * SUBLANES:(k + 1) * SUBLANES] for k in range(n_tiles)]
    for i, j in _batcher_network(n_tiles):
        v[i], v[j] = jnp.maximum(v[i], v[j]), jnp.minimum(v[i], v[j])
    vals = []
    for r in range(count):
        m = jnp.max(v[0], axis=0, keepdims=True)
        vals.append(m)
        hit = v[0] == m
        for k in range(count - r - 1):
            v[k] = jnp.where(hit, v[k + 1], v[k])
    return jnp.concatenate(vals, axis=0)


def _rank_products(a, b, sub8):
    cands = [b * a[0:1], b[0:8] * a[1:2]]
    for i in range(2, 8):
        cands.append(jnp.where(sub8 < (PEER_TOPK // (i + 1)), b[0:8] * a[i:i + 1], 0.0))
    cands.append(a[8:16] * b[0:1])
    return jnp.concatenate(cands, axis=0)


def _route_kernel(hn2_ref, wqt_ref, keys_ref, e1_ref, e2_ref, th_ref, qt_scr):
    tn = hn2_ref.shape[0]
    qt_scr[...] = lax.dot_general(wqt_ref[...], hn2_ref[...], NT_DIMS, preferred_element_type=F32)
    sub8 = lax.broadcasted_iota(jnp.int32, (SUBLANES, LANES), 0)

    def head_body(h, carry):
        q0 = pl.multiple_of(h * PEER_DK, PEER_DK)
        s1_all = _bdot(keys_ref[0], qt_scr[pl.ds(q0, PEER_HALF), :])
        s2_all = _bdot(keys_ref[1], qt_scr[pl.ds(q0 + PEER_HALF, PEER_HALF), :])
        th_rows = []
        for lg in range(tn // LANES):
            l0 = lg * LANES
            s1 = s1_all[:, l0:l0 + LANES]
            s2 = s2_all[:, l0:l0 + LANES]
            x1 = jnp.exp(s1 - jnp.max(s1, axis=0, keepdims=True))
            x2 = jnp.exp(s2 - jnp.max(s2, axis=0, keepdims=True))
            a = _top_desc_tiles(x1, PEER_TOPK)
            b = _top_desc_tiles(x2, PEER_TOPK)
            cand = _rank_products(a, b, sub8)
            theta = _top_desc(cand, PEER_TOPK)[PEER_TOPK - 1:PEER_TOPK]
            chosen = cand >= theta
            zsum = jnp.sum(jnp.where(chosen, cand, 0.0), axis=0, keepdims=True)
            rz = 0.5 / zsum
            cand_n = _rank_products(a * rz, b, sub8)
            th_rows.append(jnp.min(jnp.where(chosen, cand_n, jnp.float32(jnp.inf)), axis=0, keepdims=True))
            e1_ref[h, :, l0:l0 + LANES] = x1 * rz
            e2_ref[h, :, l0:l0 + LANES] = x2
        th_ref[pl.ds(h, 1), :] = jnp.concatenate(th_rows, axis=1)
        return carry

    lax.fori_loop(0, PEER_HEADS, head_body, 0)


def _route(hn2, wqt, keys, tn):
    t = hn2.shape[0]
    sc_spec = pl.BlockSpec((PEER_HEADS, N_KEYS, tn), lambda i: (0, 0, i))
    sc_shape = jax.ShapeDtypeStruct((PEER_HEADS, N_KEYS, t), F32)
    return pl.pallas_call(
        _route_kernel,
        grid=(t // tn,),
        in_specs=[
            pl.BlockSpec((tn, D_MODEL), lambda i: (i, 0)),
            pl.BlockSpec(wqt.shape, lambda i: (0, 0)),
            pl.BlockSpec(keys.shape, lambda i: (0, 0, 0)),
        ],
        out_specs=[sc_spec, sc_spec, pl.BlockSpec((PEER_HEADS, tn), lambda i: (0, i))],
        out_shape=[sc_shape, sc_shape, jax.ShapeDtypeStruct((PEER_HEADS, t), F32)],
        scratch_shapes=[pltpu.VMEM((PEER_HEADS * PEER_DK, tn), F32)],
        compiler_params=pltpu.CompilerParams(
            dimension_semantics=("arbitrary",), vmem_limit_bytes=VMEM_LIMIT),
        name="route",
    )(hn2, wqt, keys)


PEER_ROWS = 16


PEER_LANES = 256


def _peer_step(gate_tile, hn2_scr, u_ref, vt_ref, u_scr, vt_scr, e1_ref, e2_scr, yt_scr, e1b_scr, thb_scr,
               act_write, act_read, w_write, w_read):
    te = u_ref.shape[0]
    tm = hn2_scr.shape[0]
    n_e1 = te // N_KEYS
    rep = PEER_ROWS // SUBLANES

    u_scr[...] = u_ref[...]
    vt_scr[...] = vt_ref[...]

    n_groups = N_KEYS // PEER_ROWS
    c_rows = D_MODEL // n_e1

    def trip(i, carry):
        for h in range(PEER_HEADS):
            e1b_scr[h] = jnp.broadcast_to(e1_ref[h, pl.ds(gate_tile * n_e1 + i, 1), :], (SUBLANES, tm))

        a_base = pl.multiple_of(i * N_KEYS, N_KEYS)
        act_write[pl.ds(a_base, N_KEYS), :] = lax.dot_general(
            u_scr[pl.ds(a_base, N_KEYS), :], hn2_scr[...], NT_DIMS, preferred_element_type=F32)
        c_base = pl.multiple_of(i * c_rows, c_rows)
        yt_scr[pl.ds(c_base, c_rows), :] += jnp.dot(
            vt_scr[pl.ds(c_base, c_rows), :], w_read[...], preferred_element_type=F32)

        for lb in range(tm // PEER_LANES):
            lanes = slice(lb * PEER_LANES, (lb + 1) * PEER_LANES)
            accs = [None] * n_groups
            for h in range(PEER_HEADS):
                e1b = jnp.concatenate([e1b_scr[h, :, lanes]] * rep, axis=0)
                thb = jnp.concatenate([thb_scr[h, :, lanes]] * rep, axis=0)
                for r in range(n_groups):
                    val = e2_scr[h, r * PEER_ROWS:(r + 1) * PEER_ROWS, lanes] * e1b
                    sel = jnp.where(val >= thb, val, 0.0)
                    accs[r] = sel if accs[r] is None else accs[r] + sel
            for r in range(n_groups):
                rows = pl.ds(pl.multiple_of(a_base + r * PEER_ROWS, PEER_ROWS), PEER_ROWS)
                a = act_read[rows, lanes]
                w_write[rows, lanes] = (accs[r] * (a * (1.0 + lax.erf(a * (2.0 ** -0.5))))).astype(BF16)
        return carry

    lax.fori_loop(0, n_e1, trip, 0)


def _peer_kernel(hn2_ref, u_ref, vt_ref, e1_ref, e2_ref, th_ref, x1_ref, nw_ref,
                 out_ref, yt_scr, act0_scr, act1_scr, w0_scr, w1_scr, e1b_scr, thb_scr, e2_scr,
                 hn2_scr, u_scr, vt_scr,
                 *, n_tiles, n_steps):
    s = pl.program_id(0)
    tm = hn2_ref.shape[0]
    gate_tile = jnp.clip(s - 1, 0, n_steps - 1) % n_tiles
    lag_tile = jnp.clip(s - 2, 0, n_steps - 1) % n_tiles

    @pl.when(s == 0)
    def _():
        act1_scr[...] = jnp.zeros(act1_scr.shape, F32)
        w1_scr[...] = jnp.zeros(w1_scr.shape, BF16)

    @pl.when(lag_tile == 0)
    def _():
        yt_scr[...] = jnp.zeros(yt_scr.shape, F32)

    @pl.when(gate_tile == 0)
    def _():
        for h in range(PEER_HEADS):
            thb_scr[h] = jnp.broadcast_to(th_ref[h:h + 1, :], (SUBLANES, tm))
            e2_scr[h] = e2_ref[h]

    @pl.when(jnp.minimum(s, n_steps - 1) % n_tiles == 0)
    def _():
        hn2_scr[...] = hn2_ref[...]

    args = (gate_tile, hn2_scr, u_ref, vt_ref, u_scr, vt_scr, e1_ref, e2_scr, yt_scr, e1b_scr, thb_scr)

    @pl.when(s % 2 == 0)
    def _():
        _peer_step(*args, act0_scr, act1_scr, w0_scr, w1_scr)

    @pl.when(s % 2 == 1)
    def _():
        _peer_step(*args, act1_scr, act0_scr, w1_scr, w0_scr)

    @pl.when((lag_tile == n_tiles - 1) & (s >= 2))
    def _():
        x = x1_ref[...] + yt_scr[...].T
        out_ref[...] = x * lax.rsqrt(jnp.mean(x * x, axis=-1, keepdims=True) + EPS) * nw_ref[...]


def _peer(hn2, u_bf, vt_bf, e1, e2, th, x1, nw, tm, te):
    t = hn2.shape[0]
    n_tiles = N_EXPERTS // te
    n_steps = (t // tm) * n_tiles
    assert tm % PEER_LANES == 0 and te % N_KEYS == 0 and D_MODEL % (te // N_KEYS) == 0
    tile_a = lambda s: jnp.minimum(s, n_steps - 1)
    tile_b = lambda s: jnp.clip(s - 1, 0, n_steps - 1)
    tile_c = lambda s: jnp.clip(s - 2, 0, n_steps - 1)
    sc_spec = pl.BlockSpec((PEER_HEADS, N_KEYS, tm), lambda s: (0, 0, tile_b(s) // n_tiles))
    return pl.pallas_call(
        functools.partial(_peer_kernel, n_tiles=n_tiles, n_steps=n_steps),
        grid=(n_steps + 2,),
        in_specs=[
            pl.BlockSpec((tm, D_MODEL), lambda s: (tile_a(s) // n_tiles, 0)),
            pl.BlockSpec((te, D_MODEL), lambda s: (tile_a(s) % n_tiles, 0)),
            pl.BlockSpec((None, D_MODEL, te), lambda s: (tile_c(s) % n_tiles, 0, 0)),
            sc_spec, sc_spec,
            pl.BlockSpec((PEER_HEADS, tm), lambda s: (0, tile_b(s) // n_tiles)),
            pl.BlockSpec((tm, D_MODEL), lambda s: (tile_c(s) // n_tiles, 0)),
            pl.BlockSpec((1, D_MODEL), lambda s: (0, 0)),
        ],
        out_specs=pl.BlockSpec((tm, D_MODEL), lambda s: (tile_c(s) // n_tiles, 0)),
        out_shape=jax.ShapeDtypeStruct((t, D_MODEL), F32),
        scratch_shapes=[
            pltpu.VMEM((D_MODEL, tm), F32),
            pltpu.VMEM((te, tm), F32),
            pltpu.VMEM((te, tm), F32),
            pltpu.VMEM((te, tm), BF16),
            pltpu.VMEM((te, tm), BF16),
            pltpu.VMEM((PEER_HEADS, SUBLANES, tm), F32),
            pltpu.VMEM((PEER_HEADS, SUBLANES, tm), F32),
            pltpu.VMEM((PEER_HEADS, N_KEYS, tm), F32),
            pltpu.VMEM((tm, D_MODEL), BF16),
            pltpu.VMEM((te, D_MODEL), BF16),
            pltpu.VMEM((D_MODEL, te), BF16),
        ],
        compiler_params=pltpu.CompilerParams(
            dimension_semantics=("arbitrary",), vmem_limit_bytes=VMEM_LIMIT),
        name="peer",
    )(hn2, u_bf, vt_bf, e1, e2, th, x1, nw)


def _pick(n, pref):
    t = min(pref, n)
    while n % t:
        t //= 2
    return t


def _expansion(src_col0, n_heads, width):
    r = jnp.arange(LANES)[:, None]
    c = jnp.arange(n_heads * width)[None, :]
    return (r == src_col0 + c // width).astype(BF16)


def _chunk_masks(rows):
    r = jnp.arange(rows)[:, None]
    c = jnp.arange(rows)[None, :]
    same = (r // CHUNK) == (c // CHUNK)
    return ((same & (r >= c)).astype(BF16), (same & (r <= c)).astype(BF16), same.astype(BF16))


TILE_INPROJ = 512
TILE_MIXER = 256
TILE_ROUTE = 512
TILE_PEER_TOKENS = 512
TILE_PEER_EXPERTS = 1024


def kernel(x, norm_mix_w, w_in, ssd_conv_w, ssd_conv_b, ssd_dt_bias, ssd_a_log, ssd_d, ssd_norm_w,
           gdn_conv_w, gdn_dt_bias, gdn_a_log, gdn_norm_w, w_out, norm_ffn_w, peer_w_q, peer_sub_keys,
           peer_u, peer_v, norm_final_w):
    batch, seq, _ = x.shape
    assert w_in.shape[0] == 1, "single trunk layer"
    t = batch * seq
    x2 = x.reshape(t, D_MODEL)
    w_in = w_in[0]

    wbig = jnp.concatenate([w_in[:, OFF_SSD_Z:OFF_SSD_DT], w_in[:, OFF_GDN_QKV:OFF_GDN_BETA]],
                           axis=1).astype(BF16)
    wsm = jnp.concatenate([w_in[:, OFF_SSD_DT:OFF_GDN_QKV], w_in[:, OFF_GDN_BETA:IN_COLS]], axis=1)
    wsm = jnp.pad(wsm, ((0, 0), (0, LANES - wsm.shape[1]))).astype(BF16)
    zeros4 = jnp.zeros((GDN_HEADS,), F32)
    bias = jnp.concatenate([ssd_dt_bias[0].astype(F32), zeros4, gdn_dt_bias[0].astype(F32)])
    amul = jnp.concatenate([-jnp.exp(ssd_a_log[0].astype(F32)), zeros4, -jnp.exp(gdn_a_log[0].astype(F32))])
    pad = LANES - bias.shape[0]
    prow = jnp.pad(jnp.stack([bias, amul]), ((0, SUBLANES - 2), (0, pad)))
    pcol = jnp.pad(jnp.stack([bias, amul], axis=1), ((0, pad), (0, LANES - 2)))
    rows = _pick(seq, TILE_MIXER)
    tril, triu, blk = _chunk_masks(rows)
    consts = [
        ssd_conv_w[0].astype(F32), ssd_conv_b[0][None, :].astype(F32), gdn_conv_w[0].astype(F32),
        prow, pcol, tril, triu, blk,
        _expansion(SM_DT, SSD_HEADS, SSD_HEAD_DIM),
        _expansion(SM_BETA, GDN_HEADS, GDN_HEAD_V),
        _expansion(SM_ALPHA, GDN_HEADS, GDN_HEAD_V),
        jnp.repeat(ssd_d[0].astype(F32), SSD_HEAD_DIM)[None, :],
        ssd_norm_w[0][None, :].astype(F32), gdn_norm_w[0][None, :].astype(F32),
        w_out[0].astype(BF16), norm_ffn_w[0][None, :].astype(F32),
    ]

    big, small, smallt = _inproj(x2, norm_mix_w[0][None, :].astype(F32), wbig, wsm, wsm.T,
                                 _pick(t, TILE_INPROJ))
    x1, hn2 = _mixer(big, small, smallt, x2, consts, batch, seq, rows)
    e1, e2, th = _route(hn2, peer_w_q[0].T.astype(BF16), peer_sub_keys[0].astype(BF16),
                        _pick(t, TILE_ROUTE))
    vt_tiles = peer_v[0].astype(BF16).reshape(N_EXPERTS // TILE_PEER_EXPERTS, TILE_PEER_EXPERTS,
                                               D_MODEL).transpose(0, 2, 1)
    out = _peer(hn2, peer_u[0].astype(BF16), vt_tiles, e1, e2, th, x1,
                norm_final_w[None, :].astype(F32), _pick(t, TILE_PEER_TOKENS), TILE_PEER_EXPERTS)
    return out.reshape(batch, seq, D_MODEL)
```

```python
import functools
import math

import jax
import jax.numpy as jnp
from jax import lax
from jax.experimental import pallas as pl
from jax.experimental.pallas import tpu as pltpu

F32 = jnp.float32
BF16 = jnp.bfloat16

D_MODEL = 1024
CHUNK = 64
CONV_W = 4
EPS = 1e-6

SSD_HEADS = 8
SSD_HEAD_DIM = 64
SSD_WIDTH = SSD_HEADS * SSD_HEAD_DIM
SSD_GROUPS = 2
SSD_STATE = 128
SSD_BC = SSD_GROUPS * SSD_STATE
SSD_CONV_CH = SSD_WIDTH + 2 * SSD_BC
SSD_GROUP_W = SSD_WIDTH // SSD_GROUPS
SSD_GROUP_HEADS = SSD_HEADS // SSD_GROUPS

GDN_HEADS = 4
GDN_HEAD_K = 128
GDN_HEAD_V = 128
GDN_KEY_WIDTH = GDN_HEADS * GDN_HEAD_K
GDN_WIDTH = GDN_HEADS * GDN_HEAD_V
GDN_CONV_CH = 2 * GDN_KEY_WIDTH + GDN_WIDTH

MIX_WIDTH = SSD_WIDTH + GDN_WIDTH

OFF_SSD_Z = 0
OFF_SSD_XBC = OFF_SSD_Z + SSD_WIDTH
OFF_SSD_DT = OFF_SSD_XBC + SSD_CONV_CH
OFF_GDN_QKV = OFF_SSD_DT + SSD_HEADS
OFF_GDN_GATE = OFF_GDN_QKV + GDN_CONV_CH
OFF_GDN_BETA = OFF_GDN_GATE + GDN_WIDTH
OFF_GDN_ALPHA = OFF_GDN_BETA + GDN_HEADS
IN_COLS = OFF_GDN_ALPHA + GDN_HEADS

PEER_HEADS = 8
PEER_DK = 256
PEER_HALF = PEER_DK // 2
N_KEYS = 128
N_EXPERTS = N_KEYS * N_KEYS
PEER_TOPK = 16

LANES = 128
SUBLANES = 8
SM_DT = 0
SM_BETA = SM_DT + SSD_HEADS
SM_ALPHA = SM_BETA + GDN_HEADS
BIG_Z = 0
BIG_XBC = BIG_Z + SSD_WIDTH
BIG_QKV = BIG_XBC + SSD_CONV_CH
BIG_GATE = BIG_QKV + GDN_CONV_CH
BIG_COLS = BIG_GATE + GDN_WIDTH

CONV_PAD = 8
VMEM_LIMIT = 56 * 1024 * 1024

NT_DIMS = (((1,), (1,)), ((), ()))
TN_DIMS = (((0,), (0,)), ((), ()))


def _softplus(x):
    return jnp.maximum(x, 0.0) + jnp.log1p(jnp.exp(-jnp.abs(x)))


def _silu(x):
    return x * jax.nn.sigmoid(x)


def _bdot(a, b):
    return jnp.dot(a.astype(BF16), b.astype(BF16), preferred_element_type=F32)


def _bdot_nt(a, b):
    return lax.dot_general(a.astype(BF16), b.astype(BF16), NT_DIMS, preferred_element_type=F32)


def _bdot_tn(a, b):
    return lax.dot_general(a.astype(BF16), b.astype(BF16), TN_DIMS, preferred_element_type=F32)


def _split3(x):
    hi = x.astype(BF16)
    r = x - hi.astype(F32)
    mid = r.astype(BF16)
    lo = (r - mid.astype(F32)).astype(BF16)
    return hi, mid, lo


def _sel_dot(x, m01):
    hi, mid, lo = _split3(x)
    d = lambda a: jnp.dot(a, m01, preferred_element_type=F32)
    return d(hi) + d(mid) + d(lo)


def _sel_dot_l(m01, x):
    hi, mid, lo = _split3(x)
    d = lambda a: jnp.dot(m01, a, preferred_element_type=F32)
    return d(hi) + d(mid) + d(lo)


def _inproj_kernel(x_ref, nw_ref, wbig_ref, wsm_ref, wsmt_ref, big_ref, small_ref, smallt_ref):
    x = x_ref[...]
    h = x * lax.rsqrt(jnp.mean(x * x, axis=-1, keepdims=True) + EPS) * nw_ref[...]
    hb = h.astype(BF16)
    big_ref[...] = jnp.dot(hb, wbig_ref[...], preferred_element_type=F32)
    small_ref[...] = jnp.dot(hb, wsm_ref[...], preferred_element_type=F32)
    smallt_ref[...] = lax.dot_general(wsmt_ref[...], hb, NT_DIMS, preferred_element_type=F32)


def _inproj(x2, nw, wbig, wsm, wsmt, tm):
    t = x2.shape[0]
    const = lambda i: (0, 0)
    return pl.pallas_call(
        _inproj_kernel,
        grid=(t // tm,),
        in_specs=[
            pl.BlockSpec((tm, D_MODEL), lambda i: (i, 0)),
            pl.BlockSpec((1, D_MODEL), const),
            pl.BlockSpec((D_MODEL, BIG_COLS), const),
            pl.BlockSpec((D_MODEL, LANES), const),
            pl.BlockSpec((LANES, D_MODEL), const),
        ],
        out_specs=[
            pl.BlockSpec((tm, BIG_COLS), lambda i: (i, 0)),
            pl.BlockSpec((tm, LANES), lambda i: (i, 0)),
            pl.BlockSpec((LANES, tm), lambda i: (0, i)),
        ],
        out_shape=[
            jax.ShapeDtypeStruct((t, BIG_COLS), F32),
            jax.ShapeDtypeStruct((t, LANES), F32),
            jax.ShapeDtypeStruct((LANES, t), F32),
        ],
        compiler_params=pltpu.CompilerParams(
            dimension_semantics=("arbitrary",), vmem_limit_bytes=VMEM_LIMIT),
        name="inproj",
    )(x2, nw, wbig, wsm, wsmt)


def _causal_conv(ext_ref, u, w_ref, rows):
    ext_ref[CONV_PAD:CONV_PAD + rows, :] = u
    base = CONV_PAD - (CONV_W - 1)
    acc = w_ref[0:1, :] * ext_ref[base:base + rows, :]
    for j in range(1, CONV_W):
        acc = acc + w_ref[j:j + 1, :] * ext_ref[base + j:base + j + rows, :]
    ext_ref[base:CONV_PAD, :] = ext_ref[rows + base:rows + CONV_PAD, :]
    return acc


def _unit_lower_inverses(n_mats, eye):
    ps = [eye + n for n in n_mats]
    nks = list(n_mats)
    for _ in range(int(math.log2(CHUNK)) - 1):
        nks = [_bdot(nk, nk) for nk in nks]
        ps = [p + _bdot(p, nk) for p, nk in zip(ps, nks)]
    return ps


def _mixer_kernel(big_ref, small_ref, smallt_ref, x_ref,
                  scw_ref, scb_ref, gcw_ref, prow_ref, pcol_ref,
                  tril_ref, triu_ref, blk_ref, es_ref, eb_ref, eg_ref,
                  drow_ref, snw_ref, gnw_ref, wout_ref, nfw_ref,
                  x1_ref, hn2_ref,
                  ext_s, ext_g, sstate, gstate, ybuf, obuf):
    rows = x_ref.shape[0]
    n_chunks = rows // CHUNK

    @pl.when(pl.program_id(1) == 0)
    def _():
        ext_s[0:CONV_PAD, :] = jnp.zeros((CONV_PAD, SSD_CONV_CH), F32)
        ext_g[0:CONV_PAD, :] = jnp.zeros((CONV_PAD, GDN_CONV_CH), F32)
        sstate[...] = jnp.zeros(sstate.shape, F32)
        gstate[...] = jnp.zeros(gstate.shape, F32)

    sm = small_ref[...]
    sp = _softplus(sm + prow_ref[0:1, :])
    d_a = sp * prow_ref[1:2, :]
    beta = jax.nn.sigmoid(sm)
    smt = smallt_ref[...]
    spt = _softplus(smt + pcol_ref[:, 0:1])
    d_at = spt * pcol_ref[:, 1:2]

    cum = _sel_dot_l(tril_ref[...], d_a)
    tot = _sel_dot_l(blk_ref[...], d_a)
    cumt = _sel_dot(d_at, triu_ref[...])
    e_cum = jnp.exp(cum)
    e_end = jnp.exp(tot - cum)

    xw_scale = _sel_dot(sp * e_end, es_ref[...])
    yoff_scale = _sel_dot(e_cum, es_ref[...])
    beta_x = _sel_dot(beta, eb_ref[...])
    egc_x = _sel_dot(e_cum, eg_ref[...])
    ekd_x = _sel_dot(e_end, eg_ref[...])

    xbc = _silu(_causal_conv(ext_s, big_ref[:, BIG_XBC:BIG_QKV], scw_ref, rows) + scb_ref[...])
    qkv = _silu(_causal_conv(ext_g, big_ref[:, BIG_QKV:BIG_GATE], gcw_ref, rows))

    xs = xbc[:, :SSD_WIDTH]
    xw = xs * xw_scale

    def l2n(t):
        return t * lax.rsqrt(jnp.sum(t * t, axis=-1, keepdims=True) + EPS)

    q_parts, k_parts = [], []
    for hh in range(GDN_HEADS):
        q_parts.append(l2n(qkv[:, hh * GDN_HEAD_K:(hh + 1) * GDN_HEAD_K]))
        k_parts.append(l2n(qkv[:, GDN_KEY_WIDTH + hh * GDN_HEAD_K:GDN_KEY_WIDTH + (hh + 1) * GDN_HEAD_K]))
    q_all = jnp.concatenate(q_parts, axis=1) * (GDN_HEAD_K ** -0.5)
    k_all = jnp.concatenate(k_parts, axis=1)
    v_all = qkv[:, 2 * GDN_KEY_WIDTH:]
    kb = k_all * beta_x
    vb = v_all * beta_x
    kbg = kb * egc_x
    q_dec = q_all * egc_x
    k_dec = k_all * ekd_x

    li = lax.broadcasted_iota(jnp.int32, (CHUNK, CHUNK), 0)
    si = lax.broadcasted_iota(jnp.int32, (CHUNK, CHUNK), 1)
    incl = li >= si
    strict = li > si
    eye = jnp.where(li == si, 1.0, 0.0).astype(F32)
    neg_inf = jnp.float32(-jnp.inf)

    def decay(col, r0):
        seg = cum[r0:r0 + CHUNK, col:col + 1] - cumt[col:col + 1, r0:r0 + CHUNK]
        return jnp.exp(jnp.where(incl, seg, neg_inf))

    chunks = [(c, c * CHUNK) for c in range(n_chunks)]

    b_ms, c_ms = {}, {}
    for c, r0 in chunks:
        for g in range(SSD_GROUPS):
            b_ms[c, g] = xbc[r0:r0 + CHUNK, SSD_WIDTH + g * SSD_STATE:SSD_WIDTH + (g + 1) * SSD_STATE]
            c_ms[c, g] = xbc[r0:r0 + CHUNK,
                             SSD_WIDTH + SSD_BC + g * SSD_STATE:SSD_WIDTH + SSD_BC + (g + 1) * SSD_STATE]
    cbs = {k: _bdot_nt(c_ms[k], b_ms[k]) for k in b_ms}
    for c, r0 in chunks:
        for hh in range(SSD_HEADS):
            col = SM_DT + hh
            m_h = cbs[c, hh // SSD_GROUP_HEADS] * decay(col, r0) * spt[col:col + 1, r0:r0 + CHUNK]
            h0 = hh * SSD_HEAD_DIM
            ybuf[r0:r0 + CHUNK, h0:h0 + SSD_HEAD_DIM] = _bdot(m_h, xs[r0:r0 + CHUNK, h0:h0 + SSD_HEAD_DIM])

    decs, kqs = {}, {}
    for c, r0 in chunks:
        for hh in range(GDN_HEADS):
            k0 = hh * GDN_HEAD_K
            decs[c, hh] = decay(SM_ALPHA + hh, r0)
            lhs = jnp.concatenate([kb[r0:r0 + CHUNK, k0:k0 + GDN_HEAD_K],
                                   q_all[r0:r0 + CHUNK, k0:k0 + GDN_HEAD_K]], axis=0)
            kqs[c, hh] = _bdot_nt(lhs, k_all[r0:r0 + CHUNK, k0:k0 + GDN_HEAD_K])
    keys = list(decs)
    n_mats = [jnp.where(strict, -(kqs[k][:CHUNK] * decs[k]), 0.0) for k in keys]
    t_mats = dict(zip(keys, _unit_lower_inverses(n_mats, eye)))
    uws = {}
    for c, r0 in chunks:
        for hh in range(GDN_HEADS):
            k0 = hh * GDN_HEAD_K
            rhs = jnp.concatenate([vb[r0:r0 + CHUNK, k0:k0 + GDN_HEAD_K],
                                   kbg[r0:r0 + CHUNK, k0:k0 + GDN_HEAD_K]], axis=1)
            uws[c, hh] = _bdot(t_mats[c, hh], rhs)

    for c, r0 in chunks:
        r1 = r0 + CHUNK
        for g in range(SSD_GROUPS):
            c0 = g * SSD_GROUP_W
            st = sstate[g]
            y_off = _bdot(c_ms[c, g], st) * yoff_scale[r0:r1, c0:c0 + SSD_GROUP_W]
            ybuf[r0:r1, c0:c0 + SSD_GROUP_W] = (ybuf[r0:r1, c0:c0 + SSD_GROUP_W] + y_off
                                                + drow_ref[:, c0:c0 + SSD_GROUP_W] * xs[r0:r1, c0:c0 + SSD_GROUP_W])
            s_new = _bdot_tn(b_ms[c, g], xw[r0:r1, c0:c0 + SSD_GROUP_W])
            sstate[g] = st * yoff_scale[r1 - 1:r1, c0:c0 + SSD_GROUP_W] + s_new
        for hh in range(GDN_HEADS):
            k0 = hh * GDN_HEAD_K
            k1 = k0 + GDN_HEAD_K
            state = gstate[hh]
            lhs = jnp.concatenate([uws[c, hh][:, GDN_HEAD_V:], q_dec[r0:r1, k0:k1]], axis=0)
            ws_qs = _bdot(lhs, state)
            v_new = uws[c, hh][:, :GDN_HEAD_V] - ws_qs[:CHUNK]
            qk = kqs[c, hh][CHUNK:] * decs[c, hh]
            obuf[r0:r1, k0:k1] = ws_qs[CHUNK:] + _bdot(qk, v_new)
            gstate[hh] = state * egc_x[r1 - 1:r1, k0:k1] + _bdot_tn(k_dec[r0:r1, k0:k1], v_new)

    z = big_ref[:, BIG_Z:BIG_XBC]
    t = ybuf[...] * _silu(z)
    y_ssd = t * lax.rsqrt(jnp.mean(t * t, axis=-1, keepdims=True) + EPS) * snw_ref[...]
    o = obuf[...]
    gate = big_ref[:, BIG_GATE:BIG_COLS]
    parts = [y_ssd]
    for hh in range(GDN_HEADS):
        k0 = hh * GDN_HEAD_V
        o_h = o[:, k0:k0 + GDN_HEAD_V]
        parts.append(o_h * lax.rsqrt(jnp.mean(o_h * o_h, axis=-1, keepdims=True) + EPS)
                     * gnw_ref[...] * _silu(gate[:, k0:k0 + GDN_HEAD_V]))
    mixed = jnp.concatenate(parts, axis=1).astype(BF16)
    x1 = x_ref[...] + jnp.dot(mixed, wout_ref[...], preferred_element_type=F32)
    x1_ref[...] = x1
    hn2 = x1 * lax.rsqrt(jnp.mean(x1 * x1, axis=-1, keepdims=True) + EPS) * nfw_ref[...]
    hn2_ref[...] = hn2.astype(BF16)


def _mixer(big, small, smallt, x2, consts, batch, seq, rows):
    t = x2.shape[0]
    nblk = seq // rows
    tok = lambda b, s: (b * nblk + s, 0)
    tokt = lambda b, s: (0, b * nblk + s)
    const = lambda b, s: (0, 0)
    return pl.pallas_call(
        _mixer_kernel,
        grid=(batch, nblk),
        in_specs=[
            pl.BlockSpec((rows, BIG_COLS), tok),
            pl.BlockSpec((rows, LANES), tok),
            pl.BlockSpec((LANES, rows), tokt),
            pl.BlockSpec((rows, D_MODEL), tok),
        ] + [pl.BlockSpec(a.shape, const) for a in consts],
        out_specs=[
            pl.BlockSpec((rows, D_MODEL), tok),
            pl.BlockSpec((rows, D_MODEL), tok),
        ],
        out_shape=[
            jax.ShapeDtypeStruct((t, D_MODEL), F32),
            jax.ShapeDtypeStruct((t, D_MODEL), BF16),
        ],
        scratch_shapes=[
            pltpu.VMEM((rows + CONV_PAD, SSD_CONV_CH), F32),
            pltpu.VMEM((rows + CONV_PAD, GDN_CONV_CH), F32),
            pltpu.VMEM((SSD_GROUPS, SSD_STATE, SSD_GROUP_W), F32),
            pltpu.VMEM((GDN_HEADS, GDN_HEAD_K, GDN_HEAD_V), F32),
            pltpu.VMEM((rows, SSD_WIDTH), F32),
            pltpu.VMEM((rows, GDN_WIDTH), F32),
        ],
        compiler_params=pltpu.CompilerParams(
            dimension_semantics=("arbitrary", "arbitrary"), vmem_limit_bytes=VMEM_LIMIT),
        name="mixer",
    )(big, small, smallt, x2, *consts)


def _top_desc(s, count):
    neg_inf = jnp.float32(-jnp.inf)
    vals = []
    for _ in range(count):
        m = jnp.max(s, axis=0, keepdims=True)
        vals.append(m)
        s = jnp.where(s == m, neg_inf, s)
    return jnp.concatenate(vals, axis=0)


def _batcher_network(n):
    def merge(lo, hi, r):
        step = r * 2
        if step < hi - lo:
            yield from merge(lo, hi, step)
            yield from merge(lo + r, hi, step)
            yield from [(i, i + r) for i in range(lo + r, hi - r, step)]
        else:
            yield (lo, lo + r)

    def sort(lo, hi):
        if hi - lo >= 1:
            mid = lo + (hi - lo) // 2
            yield from sort(lo, mid)
            yield from sort(mid + 1, hi)
            yield from merge(lo, hi, 1)

    return list(sort(0, n - 1))


def _top_desc_tiles(x, count):
    n_tiles = x.shape[0] // SUBLANES
    v = [x[k * SUBLANES:(k + 1) * SUBLANES] for k in range(n_tiles)]
    for i, j in _batcher_network(n_tiles):
        v[i], v[j] = jnp.maximum(v[i], v[j]), jnp.minimum(v[i], v[j])
    vals = []
    for r in range(count):
        m = jnp.max(v[0], axis=0, keepdims=True)
        vals.append(m)
        hit = v[0] == m
        for k in range(count - r - 1):
            v[k] = jnp.where(hit, v[k + 1], v[k])
    return jnp.concatenate(vals, axis=0)


def _rank_products(a, b, sub8):
    cands = [b * a[0:1], b[0:8] * a[1:2]]
    for i in range(2, 8):
        cands.append(jnp.where(sub8 < (PEER_TOPK // (i + 1)), b[0:8] * a[i:i + 1], 0.0))
    cands.append(a[8:16] * b[0:1])
    return jnp.concatenate(cands, axis=0)


def _route_kernel(hn2_ref, wqt_ref, keys_ref, e1_ref, e2_ref, th_ref, qt_scr):
    tn = hn2_ref.shape[0]
    qt_scr[...] = lax.dot_general(wqt_ref[...], hn2_ref[...], NT_DIMS, preferred_element_type=F32)
    sub8 = lax.broadcasted_iota(jnp.int32, (SUBLANES, LANES), 0)

    def head_body(h, carry):
        q0 = pl.multiple_of(h * PEER_DK, PEER_DK)
        s1_all = _bdot(keys_ref[0], qt_scr[pl.ds(q0, PEER_HALF), :])
        s2_all = _bdot(keys_ref[1], qt_scr[pl.ds(q0 + PEER_HALF, PEER_HALF), :])
        th_rows = []
        for lg in range(tn // LANES):
            l0 = lg * LANES
            s1 = s1_all[:, l0:l0 + LANES]
            s2 = s2_all[:, l0:l0 + LANES]
            x1 = jnp.exp(s1 - jnp.max(s1, axis=0, keepdims=True))
            x2 = jnp.exp(s2 - jnp.max(s2, axis=0, keepdims=True))
            a = _top_desc_tiles(x1, PEER_TOPK)
            b = _top_desc_tiles(x2, PEER_TOPK)
            cand = _rank_products(a, b, sub8)
            theta = _top_desc(cand, PEER_TOPK)[PEER_TOPK - 1:PEER_TOPK]
            chosen = cand >= theta
            zsum = jnp.sum(jnp.where(chosen, cand, 0.0), axis=0, keepdims=True)
            rz = 0.5 / zsum
            cand_n = _rank_products(a * rz, b, sub8)
            th_rows.append(jnp.min(jnp.where(chosen, cand_n, jnp.float32(jnp.inf)), axis=0, keepdims=True))
            e1_ref[h, :, l0:l0 + LANES] = x1 * rz
            e2_ref[h, :, l0:l0 + LANES] = x2
        th_ref[pl.ds(h, 1), :] = jnp.concatenate(th_rows, axis=1)
        return carry

    lax.fori_loop(0, PEER_HEADS, head_body, 0)


def _route(hn2, wqt, keys, tn):
    t = hn2.shape[0]
    sc_spec = pl.BlockSpec((PEER_HEADS, N_KEYS, tn), lambda i: (0, 0, i))
    sc_shape = jax.ShapeDtypeStruct((PEER_HEADS, N_KEYS, t), F32)
    return pl.pallas_call(
        _route_kernel,
        grid=(t // tn,),
        in_specs=[
            pl.BlockSpec((tn, D_MODEL), lambda i: (i, 0)),
            pl.BlockSpec(wqt.shape, lambda i: (0, 0)),
            pl.BlockSpec(keys.shape, lambda i: (0, 0, 0)),
        ],
        out_specs=[sc_spec, sc_spec, pl.BlockSpec((PEER_HEADS, tn), lambda i: (0, i))],
        out_shape=[sc_shape, sc_shape, jax.ShapeDtypeStruct((PEER_HEADS, t), F32)],
        scratch_shapes=[pltpu.VMEM((PEER_HEADS * PEER_DK, tn), F32)],
        compiler_params=pltpu.CompilerParams(
            dimension_semantics=("arbitrary",), vmem_limit_bytes=VMEM_LIMIT),
        name="route",
    )(hn2, wqt, keys)


PEER_ROWS = 16


PEER_LANES = 256
MXU_N = 256
PEER_PIECE_ROWS = 256


def _peer_step(gate_tile, hn2t_scr, u_ref, vt_ref, u_scr, vt_scr, e1_ref, e2_scr, yt_scr, e1b_scr, thb_scr,
               act_write, act_read, w_write, w_read):
    te = u_ref.shape[0]
    tm = hn2t_scr.shape[1]
    n_e1 = te // N_KEYS
    rep = PEER_ROWS // SUBLANES

    u_scr[...] = u_ref[...]
    vt_scr[...] = vt_ref[...]

    def first_matmul(m, n):
        rows = slice(m * PEER_PIECE_ROWS, (m + 1) * PEER_PIECE_ROWS)
        act_write[rows, n * MXU_N:(n + 1) * MXU_N] = jnp.dot(
            u_scr[rows, :], hn2t_scr[:, n * MXU_N:(n + 1) * MXU_N], preferred_element_type=F32)

    def second_matmul(m, n):
        rows = slice(m * PEER_PIECE_ROWS, (m + 1) * PEER_PIECE_ROWS)
        yt_scr[rows, n * MXU_N:(n + 1) * MXU_N] += jnp.dot(
            vt_scr[rows, :], w_read[:, n * MXU_N:(n + 1) * MXU_N], preferred_element_type=F32)

    for e1l in range(n_e1):
        for h in range(PEER_HEADS):
            e1b_scr[e1l, h] = jnp.broadcast_to(
                e1_ref[h, pl.ds(gate_tile * n_e1 + e1l, 1), :], (SUBLANES, tm))

    def gates(e1l, lb):
        n_groups = N_KEYS // PEER_ROWS
        lanes = slice(lb * PEER_LANES, (lb + 1) * PEER_LANES)
        accs = [None] * n_groups
        for h in range(PEER_HEADS):
            e1b = jnp.concatenate([e1b_scr[e1l, h, :, lanes]] * rep, axis=0)
            thb = jnp.concatenate([thb_scr[h, :, lanes]] * rep, axis=0)
            for r in range(n_groups):
                val = e2_scr[h, r * PEER_ROWS:(r + 1) * PEER_ROWS, lanes] * e1b
                sel = jnp.where(val >= thb, val, 0.0)
                accs[r] = sel if accs[r] is None else accs[r] + sel
        for r in range(n_groups):
            rows = slice(e1l * N_KEYS + r * PEER_ROWS, e1l * N_KEYS + (r + 1) * PEER_ROWS)
            a = act_read[rows, lanes]
            w_write[rows, lanes] = (accs[r] * (a * (1.0 + lax.erf(a * (2.0 ** -0.5))))).astype(BF16)

    pieces = ([functools.partial(first_matmul, m, n)
               for n in range(tm // MXU_N) for m in range(te // PEER_PIECE_ROWS)]
              + [functools.partial(second_matmul, m, n)
                 for n in range(tm // MXU_N) for m in range(D_MODEL // PEER_PIECE_ROWS)])
    blocks = [(e1l, lb) for e1l in range(n_e1) for lb in range(tm // PEER_LANES)]
    per_block = -(-len(pieces) // len(blocks))
    for i, (e1l, lb) in enumerate(blocks):
        for piece in pieces[i * per_block:(i + 1) * per_block]:
            piece()
        gates(e1l, lb)
    for piece in pieces[len(blocks) * per_block:]:
        piece()


def _peer_kernel(hn2_ref, u_ref, vt_ref, e1_ref, e2_ref, th_ref, x1_ref, nw_ref,
                 out_ref, yt_scr, act0_scr, act1_scr, w0_scr, w1_scr, e1b_scr, thb_scr, e2_scr,
                 hn2t_scr, u_scr, vt_scr,
                 *, n_tiles, n_steps):
    s = pl.program_id(0)
    tm = hn2_ref.shape[0]
    gate_tile = jnp.clip(s - 1, 0, n_steps - 1) % n_tiles
    lag_tile = jnp.clip(s - 2, 0, n_steps - 1) % n_tiles

    @pl.when(s == 0)
    def _():
        act1_scr[...] = jnp.zeros(act1_scr.shape, F32)
        w1_scr[...] = jnp.zeros(w1_scr.shape, BF16)

    @pl.when(lag_tile == 0)
    def _():
        yt_scr[...] = jnp.zeros(yt_scr.shape, F32)

    @pl.when(gate_tile == 0)
    def _():
        for h in range(PEER_HEADS):
            thb_scr[h] = jnp.broadcast_to(th_ref[h:h + 1, :], (SUBLANES, tm))
            e2_scr[h] = e2_ref[h]

    @pl.when(jnp.minimum(s, n_steps - 1) % n_tiles == 0)
    def _():
        hn2t_scr[...] = hn2_ref[...].astype(F32).T.astype(BF16)

    args = (gate_tile, hn2t_scr, u_ref, vt_ref, u_scr, vt_scr, e1_ref, e2_scr, yt_scr, e1b_scr, thb_scr)

    @pl.when(s % 2 == 0)
    def _():
        _peer_step(*args, act0_scr, act1_scr, w0_scr, w1_scr)

    @pl.when(s % 2 == 1)
    def _():
        _peer_step(*args, act1_scr, act0_scr, w1_scr, w0_scr)

    @pl.when((lag_tile == n_tiles - 1) & (s >= 2))
    def _():
        x = x1_ref[...] + yt_scr[...].T
        out_ref[...] = x * lax.rsqrt(jnp.mean(x * x, axis=-1, keepdims=True) + EPS) * nw_ref[...]


def _peer(hn2, u_bf, vt_bf, e1, e2, th, x1, nw, tm, te):
    t = hn2.shape[0]
    n_tiles = N_EXPERTS // te
    n_steps = (t // tm) * n_tiles
    assert tm % MXU_N == 0 and tm % PEER_LANES == 0 and te % PEER_PIECE_ROWS == 0
    tile_a = lambda s: jnp.minimum(s, n_steps - 1)
    tile_b = lambda s: jnp.clip(s - 1, 0, n_steps - 1)
    tile_c = lambda s: jnp.clip(s - 2, 0, n_steps - 1)
    sc_spec = pl.BlockSpec((PEER_HEADS, N_KEYS, tm), lambda s: (0, 0, tile_b(s) // n_tiles))
    return pl.pallas_call(
        functools.partial(_peer_kernel, n_tiles=n_tiles, n_steps=n_steps),
        grid=(n_steps + 2,),
        in_specs=[
            pl.BlockSpec((tm, D_MODEL), lambda s: (tile_a(s) // n_tiles, 0)),
            pl.BlockSpec((te, D_MODEL), lambda s: (tile_a(s) % n_tiles, 0)),
            pl.BlockSpec((None, D_MODEL, te), lambda s: (tile_c(s) % n_tiles, 0, 0)),
            sc_spec, sc_spec,
            pl.BlockSpec((PEER_HEADS, tm), lambda s: (0, tile_b(s) // n_tiles)),
            pl.BlockSpec((tm, D_MODEL), lambda s: (tile_c(s) // n_tiles, 0)),
            pl.BlockSpec((1, D_MODEL), lambda s: (0, 0)),
        ],
        out_specs=pl.BlockSpec((tm, D_MODEL), lambda s: (tile_c(s) // n_tiles, 0)),
        out_shape=jax.ShapeDtypeStruct((t, D_MODEL), F32),
        scratch_shapes=[
            pltpu.VMEM((D_MODEL, tm), F32),
            pltpu.VMEM((te, tm), F32),
            pltpu.VMEM((te, tm), F32),
            pltpu.VMEM((te, tm), BF16),
            pltpu.VMEM((te, tm), BF16),
            pltpu.VMEM((te // N_KEYS, PEER_HEADS, SUBLANES, tm), F32),
            pltpu.VMEM((PEER_HEADS, SUBLANES, tm), F32),
            pltpu.VMEM((PEER_HEADS, N_KEYS, tm), F32),
            pltpu.VMEM((D_MODEL, tm), BF16),
            pltpu.VMEM((te, D_MODEL), BF16),
            pltpu.VMEM((D_MODEL, te), BF16),
        ],
        compiler_params=pltpu.CompilerParams(
            dimension_semantics=("arbitrary",), vmem_limit_bytes=VMEM_LIMIT),
        name="peer",
    )(hn2, u_bf, vt_bf, e1, e2, th, x1, nw)


def _pick(n, pref):
    t = min(pref, n)
    while n % t:
        t //= 2
    return t


def _expansion(src_col0, n_heads, width):
    r = jnp.arange(LANES)[:, None]
    c = jnp.arange(n_heads * width)[None, :]
    return (r == src_col0 + c // width).astype(BF16)


def _chunk_masks(rows):
    r = jnp.arange(rows)[:, None]
    c = jnp.arange(rows)[None, :]
    same = (r // CHUNK) == (c // CHUNK)
    return ((same & (r >= c)).astype(BF16), (same & (r <= c)).astype(BF16), same.astype(BF16))


TILE_INPROJ = 512
TILE_MIXER = 256
TILE_ROUTE = 512
TILE_PEER_TOKENS = 512
TILE_PEER_EXPERTS = 1024


def kernel(x, norm_mix_w, w_in, ssd_conv_w, ssd_conv_b, ssd_dt_bias, ssd_a_log, ssd_d, ssd_norm_w,
           gdn_conv_w, gdn_dt_bias, gdn_a_log, gdn_norm_w, w_out, norm_ffn_w, peer_w_q, peer_sub_keys,
           peer_u, peer_v, norm_final_w):
    batch, seq, _ = x.shape
    assert w_in.shape[0] == 1, "single trunk layer"
    t = batch * seq
    x2 = x.reshape(t, D_MODEL)
    w_in = w_in[0]

    wbig = jnp.concatenate([w_in[:, OFF_SSD_Z:OFF_SSD_DT], w_in[:, OFF_GDN_QKV:OFF_GDN_BETA]],
                           axis=1).astype(BF16)
    wsm = jnp.concatenate([w_in[:, OFF_SSD_DT:OFF_GDN_QKV], w_in[:, OFF_GDN_BETA:IN_COLS]], axis=1)
    wsm = jnp.pad(wsm, ((0, 0), (0, LANES - wsm.shape[1]))).astype(BF16)
    zeros4 = jnp.zeros((GDN_HEADS,), F32)
    bias = jnp.concatenate([ssd_dt_bias[0].astype(F32), zeros4, gdn_dt_bias[0].astype(F32)])
    amul = jnp.concatenate([-jnp.exp(ssd_a_log[0].astype(F32)), zeros4, -jnp.exp(gdn_a_log[0].astype(F32))])
    pad = LANES - bias.shape[0]
    prow = jnp.pad(jnp.stack([bias, amul]), ((0, SUBLANES - 2), (0, pad)))
    pcol = jnp.pad(jnp.stack([bias, amul], axis=1), ((0, pad), (0, LANES - 2)))
    rows = _pick(seq, TILE_MIXER)
    tril, triu, blk = _chunk_masks(rows)
    consts = [
        ssd_conv_w[0].astype(F32), ssd_conv_b[0][None, :].astype(F32), gdn_conv_w[0].astype(F32),
        prow, pcol, tril, triu, blk,
        _expansion(SM_DT, SSD_HEADS, SSD_HEAD_DIM),
        _expansion(SM_BETA, GDN_HEADS, GDN_HEAD_V),
        _expansion(SM_ALPHA, GDN_HEADS, GDN_HEAD_V),
        jnp.repeat(ssd_d[0].astype(F32), SSD_HEAD_DIM)[None, :],
        ssd_norm_w[0][None, :].astype(F32), gdn_norm_w[0][None, :].astype(F32),
        w_out[0].astype(BF16), norm_ffn_w[0][None, :].astype(F32),
    ]

    big, small, smallt = _inproj(x2, norm_mix_w[0][None, :].astype(F32), wbig, wsm, wsm.T,
                                 _pick(t, TILE_INPROJ))
    x1, hn2 = _mixer(big, small, smallt, x2, consts, batch, seq, rows)
    e1, e2, th = _route(hn2, peer_w_q[0].T.astype(BF16), peer_sub_keys[0].astype(BF16),
                        _pick(t, TILE_ROUTE))
    vt_tiles = peer_v[0].astype(BF16).reshape(N_EXPERTS // TILE_PEER_EXPERTS, TILE_PEER_EXPERTS,
                                               D_MODEL).transpose(0, 2, 1)
    out = _peer(hn2, peer_u[0].astype(BF16), vt_tiles, e1, e2, th, x1,
                norm_final_w[None, :].astype(F32), _pick(t, TILE_PEER_TOKENS), TILE_PEER_EXPERTS)
    return out.reshape(batch, seq, D_MODEL)
```

```python
import functools
import math

import jax
import jax.numpy as jnp
from jax import lax
from jax.experimental import pallas as pl
from jax.experimental.pallas import tpu as pltpu

F32 = jnp.float32
BF16 = jnp.bfloat16

D_MODEL = 1024
CHUNK = 64
CONV_W = 4
EPS = 1e-6

SSD_HEADS = 8
SSD_HEAD_DIM = 64
SSD_WIDTH = SSD_HEADS * SSD_HEAD_DIM
SSD_GROUPS = 2
SSD_STATE = 128
SSD_BC = SSD_GROUPS * SSD_STATE
SSD_CONV_CH = SSD_WIDTH + 2 * SSD_BC
SSD_GROUP_W = SSD_WIDTH // SSD_GROUPS
SSD_GROUP_HEADS = SSD_HEADS // SSD_GROUPS

GDN_HEADS = 4
GDN_HEAD_K = 128
GDN_HEAD_V = 128
GDN_KEY_WIDTH = GDN_HEADS * GDN_HEAD_K
GDN_WIDTH = GDN_HEADS * GDN_HEAD_V
GDN_CONV_CH = 2 * GDN_KEY_WIDTH + GDN_WIDTH

MIX_WIDTH = SSD_WIDTH + GDN_WIDTH

OFF_SSD_Z = 0
OFF_SSD_XBC = OFF_SSD_Z + SSD_WIDTH
OFF_SSD_DT = OFF_SSD_XBC + SSD_CONV_CH
OFF_GDN_QKV = OFF_SSD_DT + SSD_HEADS
OFF_GDN_GATE = OFF_GDN_QKV + GDN_CONV_CH
OFF_GDN_BETA = OFF_GDN_GATE + GDN_WIDTH
OFF_GDN_ALPHA = OFF_GDN_BETA + GDN_HEADS
IN_COLS = OFF_GDN_ALPHA + GDN_HEADS

PEER_HEADS = 8
PEER_DK = 256
PEER_HALF = PEER_DK // 2
N_KEYS = 128
N_EXPERTS = N_KEYS * N_KEYS
PEER_TOPK = 16

LANES = 128
SUBLANES = 8
SM_DT = 0
SM_BETA = SM_DT + SSD_HEADS
SM_ALPHA = SM_BETA + GDN_HEADS
BIG_Z = 0
BIG_XBC = BIG_Z + SSD_WIDTH
BIG_QKV = BIG_XBC + SSD_CONV_CH
BIG_GATE = BIG_QKV + GDN_CONV_CH
BIG_COLS = BIG_GATE + GDN_WIDTH

CONV_PAD = 8
VMEM_LIMIT = 56 * 1024 * 1024

NT_DIMS = (((1,), (1,)), ((), ()))
TN_DIMS = (((0,), (0,)), ((), ()))


def _softplus(x):
    return jnp.maximum(x, 0.0) + jnp.log1p(jnp.exp(-jnp.abs(x)))


def _silu(x):
    return x * jax.nn.sigmoid(x)


def _bdot(a, b):
    return jnp.dot(a.astype(BF16), b.astype(BF16), preferred_element_type=F32)


def _bdot_nt(a, b):
    return lax.dot_general(a.astype(BF16), b.astype(BF16), NT_DIMS, preferred_element_type=F32)


def _bdot_tn(a, b):
    return lax.dot_general(a.astype(BF16), b.astype(BF16), TN_DIMS, preferred_element_type=F32)


def _split3(x):
    hi = x.astype(BF16)
    r = x - hi.astype(F32)
    mid = r.astype(BF16)
    lo = (r - mid.astype(F32)).astype(BF16)
    return hi, mid, lo


def _sel_dot(x, m01):
    hi, mid, lo = _split3(x)
    d = lambda a: jnp.dot(a, m01, preferred_element_type=F32)
    return d(hi) + d(mid) + d(lo)


def _sel_dot_l(m01, x):
    hi, mid, lo = _split3(x)
    d = lambda a: jnp.dot(m01, a, preferred_element_type=F32)
    return d(hi) + d(mid) + d(lo)


def _inproj_kernel(x_ref, nw_ref, wbig_ref, wsm_ref, wsmt_ref, big_ref, small_ref, smallt_ref):
    x = x_ref[...]
    h = x * lax.rsqrt(jnp.mean(x * x, axis=-1, keepdims=True) + EPS) * nw_ref[...]
    hb = h.astype(BF16)
    big_ref[...] = jnp.dot(hb, wbig_ref[...], preferred_element_type=F32)
    small_ref[...] = jnp.dot(hb, wsm_ref[...], preferred_element_type=F32)
    smallt_ref[...] = lax.dot_general(wsmt_ref[...], hb, NT_DIMS, preferred_element_type=F32)


def _inproj(x2, nw, wbig, wsm, wsmt, tm):
    t = x2.shape[0]
    const = lambda i: (0, 0)
    return pl.pallas_call(
        _inproj_kernel,
        grid=(t // tm,),
        in_specs=[
            pl.BlockSpec((tm, D_MODEL), lambda i: (i, 0)),
            pl.BlockSpec((1, D_MODEL), const),
            pl.BlockSpec((D_MODEL, BIG_COLS), const),
            pl.BlockSpec((D_MODEL, LANES), const),
            pl.BlockSpec((LANES, D_MODEL), const),
        ],
        out_specs=[
            pl.BlockSpec((tm, BIG_COLS), lambda i: (i, 0)),
            pl.BlockSpec((tm, LANES), lambda i: (i, 0)),
            pl.BlockSpec((LANES, tm), lambda i: (0, i)),
        ],
        out_shape=[
            jax.ShapeDtypeStruct((t, BIG_COLS), F32),
            jax.ShapeDtypeStruct((t, LANES), F32),
            jax.ShapeDtypeStruct((LANES, t), F32),
        ],
        compiler_params=pltpu.CompilerParams(
            dimension_semantics=("arbitrary",), vmem_limit_bytes=VMEM_LIMIT),
        name="inproj",
    )(x2, nw, wbig, wsm, wsmt)


def _causal_conv(ext_ref, u, w_ref, rows):
    ext_ref[CONV_PAD:CONV_PAD + rows, :] = u
    base = CONV_PAD - (CONV_W - 1)
    acc = w_ref[0:1, :] * ext_ref[base:base + rows, :]
    for j in range(1, CONV_W):
        acc = acc + w_ref[j:j + 1, :] * ext_ref[base + j:base + j + rows, :]
    ext_ref[base:CONV_PAD, :] = ext_ref[rows + base:rows + CONV_PAD, :]
    return acc


def _unit_lower_inverses(n_mats, eye):
    ps = [eye + n for n in n_mats]
    nks = list(n_mats)
    for _ in range(int(math.log2(CHUNK)) - 1):
        nks = [_bdot(nk, nk) for nk in nks]
        ps = [p + _bdot(p, nk) for p, nk in zip(ps, nks)]
    return ps


def _mixer_kernel(big_ref, small_ref, smallt_ref, x_ref,
                  scw_ref, scb_ref, gcw_ref, prow_ref, pcol_ref,
                  tril_ref, triu_ref, blk_ref, es_ref, eb_ref, eg_ref,
                  drow_ref, snw_ref, gnw_ref, wout_ref, nfw_ref,
                  x1_ref, hn2_ref,
                  ext_s, ext_g, sstate, gstate, ybuf, obuf):
    rows = x_ref.shape[0]
    n_chunks = rows // CHUNK

    @pl.when(pl.program_id(1) == 0)
    def _():
        ext_s[0:CONV_PAD, :] = jnp.zeros((CONV_PAD, SSD_CONV_CH), F32)
        ext_g[0:CONV_PAD, :] = jnp.zeros((CONV_PAD, GDN_CONV_CH), F32)
        sstate[...] = jnp.zeros(sstate.shape, F32)
        gstate[...] = jnp.zeros(gstate.shape, F32)

    sm = small_ref[...]
    sp = _softplus(sm + prow_ref[0:1, :])
    d_a = sp * prow_ref[1:2, :]
    beta = jax.nn.sigmoid(sm)
    smt = smallt_ref[...]
    spt = _softplus(smt + pcol_ref[:, 0:1])
    d_at = spt * pcol_ref[:, 1:2]

    cum = _sel_dot_l(tril_ref[...], d_a)
    tot = _sel_dot_l(blk_ref[...], d_a)
    cumt = _sel_dot(d_at, triu_ref[...])
    e_cum = jnp.exp(cum)
    e_end = jnp.exp(tot - cum)

    xw_scale = _sel_dot(sp * e_end, es_ref[...])
    yoff_scale = _sel_dot(e_cum, es_ref[...])
    beta_x = _sel_dot(beta, eb_ref[...])
    egc_x = _sel_dot(e_cum, eg_ref[...])
    ekd_x = _sel_dot(e_end, eg_ref[...])

    xbc = _silu(_causal_conv(ext_s, big_ref[:, BIG_XBC:BIG_QKV], scw_ref, rows) + scb_ref[...])
    qkv = _silu(_causal_conv(ext_g, big_ref[:, BIG_QKV:BIG_GATE], gcw_ref, rows))

    xs = xbc[:, :SSD_WIDTH]
    xw = xs * xw_scale

    def l2n(t):
        return t * lax.rsqrt(jnp.sum(t * t, axis=-1, keepdims=True) + EPS)

    q_parts, k_parts = [], []
    for hh in range(GDN_HEADS):
        q_parts.append(l2n(qkv[:, hh * GDN_HEAD_K:(hh + 1) * GDN_HEAD_K]))
        k_parts.append(l2n(qkv[:, GDN_KEY_WIDTH + hh * GDN_HEAD_K:GDN_KEY_WIDTH + (hh + 1) * GDN_HEAD_K]))
    q_all = jnp.concatenate(q_parts, axis=1) * (GDN_HEAD_K ** -0.5)
    k_all = jnp.concatenate(k_parts, axis=1)
    v_all = qkv[:, 2 * GDN_KEY_WIDTH:]
    kb = k_all * beta_x
    vb = v_all * beta_x
    kbg = kb * egc_x
    q_dec = q_all * egc_x
    k_dec = k_all * ekd_x

    li = lax.broadcasted_iota(jnp.int32, (CHUNK, CHUNK), 0)
    si = lax.broadcasted_iota(jnp.int32, (CHUNK, CHUNK), 1)
    incl = li >= si
    strict = li > si
    eye = jnp.where(li == si, 1.0, 0.0).astype(F32)
    neg_inf = jnp.float32(-jnp.inf)

    def decay(col, r0):
        seg = cum[r0:r0 + CHUNK, col:col + 1] - cumt[col:col + 1, r0:r0 + CHUNK]
        return jnp.exp(jnp.where(incl, seg, neg_inf))

    chunks = [(c, c * CHUNK) for c in range(n_chunks)]

    b_ms, c_ms = {}, {}
    for c, r0 in chunks:
        for g in range(SSD_GROUPS):
            b_ms[c, g] = xbc[r0:r0 + CHUNK, SSD_WIDTH + g * SSD_STATE:SSD_WIDTH + (g + 1) * SSD_STATE]
            c_ms[c, g] = xbc[r0:r0 + CHUNK,
                             SSD_WIDTH + SSD_BC + g * SSD_STATE:SSD_WIDTH + SSD_BC + (g + 1) * SSD_STATE]
    cbs = {k: _bdot_nt(c_ms[k], b_ms[k]) for k in b_ms}
    for c, r0 in chunks:
        for hh in range(SSD_HEADS):
            col = SM_DT + hh
            m_h = cbs[c, hh // SSD_GROUP_HEADS] * decay(col, r0) * spt[col:col + 1, r0:r0 + CHUNK]
            h0 = hh * SSD_HEAD_DIM
            ybuf[r0:r0 + CHUNK, h0:h0 + SSD_HEAD_DIM] = _bdot(m_h, xs[r0:r0 + CHUNK, h0:h0 + SSD_HEAD_DIM])

    decs, kqs = {}, {}
    for c, r0 in chunks:
        for hh in range(GDN_HEADS):
            k0 = hh * GDN_HEAD_K
            decs[c, hh] = decay(SM_ALPHA + hh, r0)
            lhs = jnp.concatenate([kb[r0:r0 + CHUNK, k0:k0 + GDN_HEAD_K],
                                   q_all[r0:r0 + CHUNK, k0:k0 + GDN_HEAD_K]], axis=0)
            kqs[c, hh] = _bdot_nt(lhs, k_all[r0:r0 + CHUNK, k0:k0 + GDN_HEAD_K])
    keys = list(decs)
    n_mats = [jnp.where(strict, -(kqs[k][:CHUNK] * decs[k]), 0.0) for k in keys]
    t_mats = dict(zip(keys, _unit_lower_inverses(n_mats, eye)))
    uws = {}
    for c, r0 in chunks:
        for hh in range(GDN_HEADS):
            k0 = hh * GDN_HEAD_K
            rhs = jnp.concatenate([vb[r0:r0 + CHUNK, k0:k0 + GDN_HEAD_K],
                                   kbg[r0:r0 + CHUNK, k0:k0 + GDN_HEAD_K]], axis=1)
            uws[c, hh] = _bdot(t_mats[c, hh], rhs)

    for c, r0 in chunks:
        r1 = r0 + CHUNK
        for g in range(SSD_GROUPS):
            c0 = g * SSD_GROUP_W
            st = sstate[g]
            y_off = _bdot(c_ms[c, g], st) * yoff_scale[r0:r1, c0:c0 + SSD_GROUP_W]
            ybuf[r0:r1, c0:c0 + SSD_GROUP_W] = (ybuf[r0:r1, c0:c0 + SSD_GROUP_W] + y_off
                                                + drow_ref[:, c0:c0 + SSD_GROUP_W] * xs[r0:r1, c0:c0 + SSD_GROUP_W])
            s_new = _bdot_tn(b_ms[c, g], xw[r0:r1, c0:c0 + SSD_GROUP_W])
            sstate[g] = st * yoff_scale[r1 - 1:r1, c0:c0 + SSD_GROUP_W] + s_new
        for hh in range(GDN_HEADS):
            k0 = hh * GDN_HEAD_K
            k1 = k0 + GDN_HEAD_K
            state = gstate[hh]
            lhs = jnp.concatenate([uws[c, hh][:, GDN_HEAD_V:], q_dec[r0:r1, k0:k1]], axis=0)
            ws_qs = _bdot(lhs, state)
            v_new = uws[c, hh][:, :GDN_HEAD_V] - ws_qs[:CHUNK]
            qk = kqs[c, hh][CHUNK:] * decs[c, hh]
            obuf[r0:r1, k0:k1] = ws_qs[CHUNK:] + _bdot(qk, v_new)
            gstate[hh] = state * egc_x[r1 - 1:r1, k0:k1] + _bdot_tn(k_dec[r0:r1, k0:k1], v_new)

    z = big_ref[:, BIG_Z:BIG_XBC]
    t = ybuf[...] * _silu(z)
    y_ssd = t * lax.rsqrt(jnp.mean(t * t, axis=-1, keepdims=True) + EPS) * snw_ref[...]
    o = obuf[...]
    gate = big_ref[:, BIG_GATE:BIG_COLS]
    parts = [y_ssd]
    for hh in range(GDN_HEADS):
        k0 = hh * GDN_HEAD_V
        o_h = o[:, k0:k0 + GDN_HEAD_V]
        parts.append(o_h * lax.rsqrt(jnp.mean(o_h * o_h, axis=-1, keepdims=True) + EPS)
                     * gnw_ref[...] * _silu(gate[:, k0:k0 + GDN_HEAD_V]))
    mixed = jnp.concatenate(parts, axis=1).astype(BF16)
    x1 = x_ref[...] + jnp.dot(mixed, wout_ref[...], preferred_element_type=F32)
    x1_ref[...] = x1
    hn2 = x1 * lax.rsqrt(jnp.mean(x1 * x1, axis=-1, keepdims=True) + EPS) * nfw_ref[...]
    hn2_ref[...] = hn2.astype(BF16)


def _mixer(big, small, smallt, x2, consts, batch, seq, rows):
    t = x2.shape[0]
    nblk = seq // rows
    tok = lambda b, s: (b * nblk + s, 0)
    tokt = lambda b, s: (0, b * nblk + s)
    const = lambda b, s: (0, 0)
    return pl.pallas_call(
        _mixer_kernel,
        grid=(batch, nblk),
        in_specs=[
            pl.BlockSpec((rows, BIG_COLS), tok),
            pl.BlockSpec((rows, LANES), tok),
            pl.BlockSpec((LANES, rows), tokt),
            pl.BlockSpec((rows, D_MODEL), tok),
        ] + [pl.BlockSpec(a.shape, const) for a in consts],
        out_specs=[
            pl.BlockSpec((rows, D_MODEL), tok),
            pl.BlockSpec((rows, D_MODEL), tok),
        ],
        out_shape=[
            jax.ShapeDtypeStruct((t, D_MODEL), F32),
            jax.ShapeDtypeStruct((t, D_MODEL), BF16),
        ],
        scratch_shapes=[
            pltpu.VMEM((rows + CONV_PAD, SSD_CONV_CH), F32),
            pltpu.VMEM((rows + CONV_PAD, GDN_CONV_CH), F32),
            pltpu.VMEM((SSD_GROUPS, SSD_STATE, SSD_GROUP_W), F32),
            pltpu.VMEM((GDN_HEADS, GDN_HEAD_K, GDN_HEAD_V), F32),
            pltpu.VMEM((rows, SSD_WIDTH), F32),
            pltpu.VMEM((rows, GDN_WIDTH), F32),
        ],
        compiler_params=pltpu.CompilerParams(
            dimension_semantics=("arbitrary", "arbitrary"), vmem_limit_bytes=VMEM_LIMIT),
        name="mixer",
    )(big, small, smallt, x2, *consts)


def _top_desc(s, count):
    neg_inf = jnp.float32(-jnp.inf)
    vals = []
    for _ in range(count):
        m = jnp.max(s, axis=0, keepdims=True)
        vals.append(m)
        s = jnp.where(s == m, neg_inf, s)
    return jnp.concatenate(vals, axis=0)


def _batcher_network(n):
    def merge(lo, hi, r):
        step = r * 2
        if step < hi - lo:
            yield from merge(lo, hi, step)
            yield from merge(lo + r, hi, step)
            yield from [(i, i + r) for i in range(lo + r, hi - r, step)]
        else:
            yield (lo, lo + r)

    def sort(lo, hi):
        if hi - lo >= 1:
            mid = lo + (hi - lo) // 2
            yield from sort(lo, mid)
            yield from sort(mid + 1, hi)
            yield from merge(lo, hi, 1)

    return list(sort(0, n - 1))


def _top_desc_tiles(x, count):
    n_tiles = x.shape[0] // SUBLANES
    v = [x[k * SUBLANES:(k + 1) * SUBLANES] for k in range(n_tiles)]
    for i, j in _batcher_network(n_tiles):
        v[i], v[j] = jnp.maximum(v[i], v[j]), jnp.minimum(v[i], v[j])
    vals = []
    for r in range(count):
        m = jnp.max(v[0], axis=0, keepdims=True)
        vals.append(m)
        hit = v[0] == m
        for k in range(count - r - 1):
            v[k] = jnp.where(hit, v[k + 1], v[k])
    return jnp.concatenate(vals, axis=0)


def _rank_products(a, b, sub8):
    cands = [b * a[0:1], b[0:8] * a[1:2]]
    for i in range(2, 8):
        cands.append(jnp.where(sub8 < (PEER_TOPK // (i + 1)), b[0:8] * a[i:i + 1], 0.0))
    cands.append(a[8:16] * b[0:1])
    return jnp.concatenate(cands, axis=0)


def _route_kernel(hn2_ref, wqt_ref, keys_ref, e1_ref, n1_ref, e2_ref, r2_ref, qt_scr):
    tn = hn2_ref.shape[0]
    qt_scr[...] = lax.dot_general(wqt_ref[...], hn2_ref[...], NT_DIMS, preferred_element_type=F32)
    sub8 = lax.broadcasted_iota(jnp.int32, (SUBLANES, LANES), 0)

    def head_body(h, carry):
        q0 = pl.multiple_of(h * PEER_DK, PEER_DK)
        s1_all = _bdot(keys_ref[0], qt_scr[pl.ds(q0, PEER_HALF), :])
        s2_all = _bdot(keys_ref[1], qt_scr[pl.ds(q0 + PEER_HALF, PEER_HALF), :])
        for lg in range(tn // LANES):
            l0 = lg * LANES
            s1 = s1_all[:, l0:l0 + LANES]
            s2 = s2_all[:, l0:l0 + LANES]
            x1 = jnp.exp(s1 - jnp.max(s1, axis=0, keepdims=True))
            x2 = jnp.exp(s2 - jnp.max(s2, axis=0, keepdims=True))
            a = _top_desc_tiles(x1, PEER_TOPK)
            b = _top_desc_tiles(x2, PEER_TOPK)
            cand = _rank_products(a, b, sub8)
            theta = _top_desc(cand, PEER_TOPK)[PEER_TOPK - 1:PEER_TOPK]
            chosen = cand >= theta
            zsum = jnp.sum(jnp.where(chosen, cand, 0.0), axis=0, keepdims=True)
            rz = 0.5 / zsum
            picked = jnp.where(chosen, 1.0, 0.0)
            counts = [jnp.sum(picked[0:PEER_TOPK], axis=0, keepdims=True)]
            for i in range(1, 8):
                r0 = PEER_TOPK + 8 * (i - 1)
                counts.append(jnp.sum(picked[r0:r0 + 8], axis=0, keepdims=True))
            n_rank = jnp.concatenate(counts + [picked[PEER_TOPK + 56:PEER_TOPK + 64]], axis=0)
            n1 = jnp.zeros_like(x1)
            r2 = jnp.full_like(x2, float(PEER_TOPK))
            for i in range(PEER_TOPK):
                n1 = jnp.where(x1 == a[i:i + 1], n_rank[i:i + 1], n1)
                r2 = jnp.where(x2 == b[i:i + 1], float(i), r2)
            e1_ref[h, :, l0:l0 + LANES] = x1 * rz
            n1_ref[h, :, l0:l0 + LANES] = n1
            e2_ref[h, :, l0:l0 + LANES] = x2.astype(BF16)
            r2_ref[h, :, l0:l0 + LANES] = r2.astype(BF16)
        return carry

    lax.fori_loop(0, PEER_HEADS, head_body, 0)


def _route(hn2, wqt, keys, tn):
    t = hn2.shape[0]
    sc_spec = pl.BlockSpec((PEER_HEADS, N_KEYS, tn), lambda i: (0, 0, i))
    sc_shape = jax.ShapeDtypeStruct((PEER_HEADS, N_KEYS, t), F32)
    sc_shape_bf = jax.ShapeDtypeStruct((PEER_HEADS, N_KEYS, t), BF16)
    return pl.pallas_call(
        _route_kernel,
        grid=(t // tn,),
        in_specs=[
            pl.BlockSpec((tn, D_MODEL), lambda i: (i, 0)),
            pl.BlockSpec(wqt.shape, lambda i: (0, 0)),
            pl.BlockSpec(keys.shape, lambda i: (0, 0, 0)),
        ],
        out_specs=[sc_spec, sc_spec, sc_spec, sc_spec],
        out_shape=[sc_shape, sc_shape, sc_shape_bf, sc_shape_bf],
        scratch_shapes=[pltpu.VMEM((PEER_HEADS * PEER_DK, tn), F32)],
        compiler_params=pltpu.CompilerParams(
            dimension_semantics=("arbitrary",), vmem_limit_bytes=VMEM_LIMIT),
        name="route",
    )(hn2, wqt, keys)


PEER_ROWS = 16


PEER_LANES = 256
MXU_N = 256
PEER_PIECE_ROWS = 256


def _peer_step(gate_tile, hn2t_scr, u_ref, vt_ref, u_scr, vt_scr, e1_ref, n1_ref, e2_scr, r2_scr, yt_scr,
               e1b_scr, n1b_scr,
               act_write, act_read, w_write, w_read):
    te = u_ref.shape[0]
    tm = hn2t_scr.shape[1]
    n_e1 = te // N_KEYS

    u_scr[...] = u_ref[...]
    vt_scr[...] = vt_ref[...]

    def first_matmul(m, n):
        rows = slice(m * PEER_PIECE_ROWS, (m + 1) * PEER_PIECE_ROWS)
        act_write[rows, n * MXU_N:(n + 1) * MXU_N] = jnp.dot(
            u_scr[rows, :], hn2t_scr[:, n * MXU_N:(n + 1) * MXU_N], preferred_element_type=F32)

    def second_matmul(m, n):
        rows = slice(m * PEER_PIECE_ROWS, (m + 1) * PEER_PIECE_ROWS)
        yt_scr[rows, n * MXU_N:(n + 1) * MXU_N] += jnp.dot(
            vt_scr[rows, :], w_read[:, n * MXU_N:(n + 1) * MXU_N], preferred_element_type=F32)

    for e1l in range(n_e1):
        for h in range(PEER_HEADS):
            row = pl.ds(gate_tile * n_e1 + e1l, 1)
            e1b_scr[e1l, h] = jnp.broadcast_to(e1_ref[h, row, :], (PEER_ROWS, tm)).astype(BF16)
            n1b_scr[e1l, h] = jnp.broadcast_to(n1_ref[h, row, :], (PEER_ROWS, tm)).astype(BF16)

    def gates(e1l, lb):
        n_groups = N_KEYS // PEER_ROWS
        lanes = slice(lb * PEER_LANES, (lb + 1) * PEER_LANES)
        accs = [None] * n_groups
        zero = jnp.zeros((PEER_ROWS, PEER_LANES), BF16)
        for h in range(PEER_HEADS):
            e1b = e1b_scr[e1l, h, :, lanes]
            n1b = n1b_scr[e1l, h, :, lanes]
            for r in range(n_groups):
                rows = slice(r * PEER_ROWS, (r + 1) * PEER_ROWS)
                sel = jnp.where(r2_scr[h, rows, lanes] < n1b, e2_scr[h, rows, lanes] * e1b, zero)
                accs[r] = sel if accs[r] is None else accs[r] + sel
        for r in range(n_groups):
            rows = slice(e1l * N_KEYS + r * PEER_ROWS, e1l * N_KEYS + (r + 1) * PEER_ROWS)
            a = act_read[rows, lanes]
            w_write[rows, lanes] = accs[r] * (a * (1.0 + lax.erf(a * (2.0 ** -0.5)))).astype(BF16)

    pieces = ([functools.partial(first_matmul, m, n)
               for n in range(tm // MXU_N) for m in range(te // PEER_PIECE_ROWS)]
              + [functools.partial(second_matmul, m, n)
                 for n in range(tm // MXU_N) for m in range(D_MODEL // PEER_PIECE_ROWS)])
    blocks = [(e1l, lb) for e1l in range(n_e1) for lb in range(tm // PEER_LANES)]
    per_block = -(-len(pieces) // len(blocks))
    for i, (e1l, lb) in enumerate(blocks):
        for piece in pieces[i * per_block:(i + 1) * per_block]:
            piece()
        gates(e1l, lb)
    for piece in pieces[len(blocks) * per_block:]:
        piece()


def _peer_kernel(hn2_ref, u_ref, vt_ref, e1_ref, n1_ref, e2_ref, r2_ref, x1_ref, nw_ref,
                 out_ref, yt_scr, act0_scr, act1_scr, w0_scr, w1_scr, e1b_scr, n1b_scr, e2_scr, r2_scr,
                 hn2t_scr, u_scr, vt_scr,
                 *, n_tiles, n_steps):
    s = pl.program_id(0)
    tm = hn2_ref.shape[0]
    gate_tile = jnp.clip(s - 1, 0, n_steps - 1) % n_tiles
    lag_tile = jnp.clip(s - 2, 0, n_steps - 1) % n_tiles

    @pl.when(s == 0)
    def _():
        act1_scr[...] = jnp.zeros(act1_scr.shape, F32)
        w1_scr[...] = jnp.zeros(w1_scr.shape, BF16)

    @pl.when(lag_tile == 0)
    def _():
        yt_scr[...] = jnp.zeros(yt_scr.shape, F32)

    @pl.when(gate_tile == 0)
    def _():
        for h in range(PEER_HEADS):
            e2_scr[h] = e2_ref[h]
            r2_scr[h] = r2_ref[h]

    @pl.when(jnp.minimum(s, n_steps - 1) % n_tiles == 0)
    def _():
        hn2t_scr[...] = hn2_ref[...].astype(F32).T.astype(BF16)

    args = (gate_tile, hn2t_scr, u_ref, vt_ref, u_scr, vt_scr, e1_ref, n1_ref, e2_scr, r2_scr, yt_scr,
            e1b_scr, n1b_scr)

    @pl.when(s % 2 == 0)
    def _():
        _peer_step(*args, act0_scr, act1_scr, w0_scr, w1_scr)

    @pl.when(s % 2 == 1)
    def _():
        _peer_step(*args, act1_scr, act0_scr, w1_scr, w0_scr)

    @pl.when((lag_tile == n_tiles - 1) & (s >= 2))
    def _():
        x = x1_ref[...] + yt_scr[...].T
        out_ref[...] = x * lax.rsqrt(jnp.mean(x * x, axis=-1, keepdims=True) + EPS) * nw_ref[...]


def _peer(hn2, u_bf, vt_bf, e1, n1, e2, r2, x1, nw, tm, te):
    t = hn2.shape[0]
    n_tiles = N_EXPERTS // te
    n_steps = (t // tm) * n_tiles
    assert tm % MXU_N == 0 and tm % PEER_LANES == 0 and te % PEER_PIECE_ROWS == 0
    tile_a = lambda s: jnp.minimum(s, n_steps - 1)
    tile_b = lambda s: jnp.clip(s - 1, 0, n_steps - 1)
    tile_c = lambda s: jnp.clip(s - 2, 0, n_steps - 1)
    sc_spec = pl.BlockSpec((PEER_HEADS, N_KEYS, tm), lambda s: (0, 0, tile_b(s) // n_tiles))
    return pl.pallas_call(
        functools.partial(_peer_kernel, n_tiles=n_tiles, n_steps=n_steps),
        grid=(n_steps + 2,),
        in_specs=[
            pl.BlockSpec((tm, D_MODEL), lambda s: (tile_a(s) // n_tiles, 0)),
            pl.BlockSpec((te, D_MODEL), lambda s: (tile_a(s) % n_tiles, 0)),
            pl.BlockSpec((None, D_MODEL, te), lambda s: (tile_c(s) % n_tiles, 0, 0)),
            sc_spec, sc_spec, sc_spec, sc_spec,
            pl.BlockSpec((tm, D_MODEL), lambda s: (tile_c(s) // n_tiles, 0)),
            pl.BlockSpec((1, D_MODEL), lambda s: (0, 0)),
        ],
        out_specs=pl.BlockSpec((tm, D_MODEL), lambda s: (tile_c(s) // n_tiles, 0)),
        out_shape=jax.ShapeDtypeStruct((t, D_MODEL), F32),
        scratch_shapes=[
            pltpu.VMEM((D_MODEL, tm), F32),
            pltpu.VMEM((te, tm), F32),
            pltpu.VMEM((te, tm), F32),
            pltpu.VMEM((te, tm), BF16),
            pltpu.VMEM((te, tm), BF16),
            pltpu.VMEM((te // N_KEYS, PEER_HEADS, PEER_ROWS, tm), BF16),
            pltpu.VMEM((te // N_KEYS, PEER_HEADS, PEER_ROWS, tm), BF16),
            pltpu.VMEM((PEER_HEADS, N_KEYS, tm), BF16),
            pltpu.VMEM((PEER_HEADS, N_KEYS, tm), BF16),
            pltpu.VMEM((D_MODEL, tm), BF16),
            pltpu.VMEM((te, D_MODEL), BF16),
            pltpu.VMEM((D_MODEL, te), BF16),
        ],
        compiler_params=pltpu.CompilerParams(
            dimension_semantics=("arbitrary",), vmem_limit_bytes=VMEM_LIMIT),
        name="peer",
    )(hn2, u_bf, vt_bf, e1, n1, e2, r2, x1, nw)


def _pick(n, pref):
    t = min(pref, n)
    while n % t:
        t //= 2
    return t


def _expansion(src_col0, n_heads, width):
    r = jnp.arange(LANES)[:, None]
    c = jnp.arange(n_heads * width)[None, :]
    return (r == src_col0 + c // width).astype(BF16)


def _chunk_masks(rows):
    r = jnp.arange(rows)[:, None]
    c = jnp.arange(rows)[None, :]
    same = (r // CHUNK) == (c // CHUNK)
    return ((same & (r >= c)).astype(BF16), (same & (r <= c)).astype(BF16), same.astype(BF16))


TILE_INPROJ = 512
TILE_MIXER = 256
TILE_ROUTE = 512
TILE_PEER_TOKENS = 512
TILE_PEER_EXPERTS = 1024


def kernel(x, norm_mix_w, w_in, ssd_conv_w, ssd_conv_b, ssd_dt_bias, ssd_a_log, ssd_d, ssd_norm_w,
           gdn_conv_w, gdn_dt_bias, gdn_a_log, gdn_norm_w, w_out, norm_ffn_w, peer_w_q, peer_sub_keys,
           peer_u, peer_v, norm_final_w):
    batch, seq, _ = x.shape
    assert w_in.shape[0] == 1, "single trunk layer"
    t = batch * seq
    x2 = x.reshape(t, D_MODEL)
    w_in = w_in[0]

    wbig = jnp.concatenate([w_in[:, OFF_SSD_Z:OFF_SSD_DT], w_in[:, OFF_GDN_QKV:OFF_GDN_BETA]],
                           axis=1).astype(BF16)
    wsm = jnp.concatenate([w_in[:, OFF_SSD_DT:OFF_GDN_QKV], w_in[:, OFF_GDN_BETA:IN_COLS]], axis=1)
    wsm = jnp.pad(wsm, ((0, 0), (0, LANES - wsm.shape[1]))).astype(BF16)
    zeros4 = jnp.zeros((GDN_HEADS,), F32)
    bias = jnp.concatenate([ssd_dt_bias[0].astype(F32), zeros4, gdn_dt_bias[0].astype(F32)])
    amul = jnp.concatenate([-jnp.exp(ssd_a_log[0].astype(F32)), zeros4, -jnp.exp(gdn_a_log[0].astype(F32))])
    pad = LANES - bias.shape[0]
    prow = jnp.pad(jnp.stack([bias, amul]), ((0, SUBLANES - 2), (0, pad)))
    pcol = jnp.pad(jnp.stack([bias, amul], axis=1), ((0, pad), (0, LANES - 2)))
    rows = _pick(seq, TILE_MIXER)
    tril, triu, blk = _chunk_masks(rows)
    consts = [
        ssd_conv_w[0].astype(F32), ssd_conv_b[0][None, :].astype(F32), gdn_conv_w[0].astype(F32),
        prow, pcol, tril, triu, blk,
        _expansion(SM_DT, SSD_HEADS, SSD_HEAD_DIM),
        _expansion(SM_BETA, GDN_HEADS, GDN_HEAD_V),
        _expansion(SM_ALPHA, GDN_HEADS, GDN_HEAD_V),
        jnp.repeat(ssd_d[0].astype(F32), SSD_HEAD_DIM)[None, :],
        ssd_norm_w[0][None, :].astype(F32), gdn_norm_w[0][None, :].astype(F32),
        w_out[0].astype(BF16), norm_ffn_w[0][None, :].astype(F32),
    ]

    big, small, smallt = _inproj(x2, norm_mix_w[0][None, :].astype(F32), wbig, wsm, wsm.T,
                                 _pick(t, TILE_INPROJ))
    x1, hn2 = _mixer(big, small, smallt, x2, consts, batch, seq, rows)
    e1, n1, e2, r2 = _route(hn2, peer_w_q[0].T.astype(BF16), peer_sub_keys[0].astype(BF16),
                        _pick(t, TILE_ROUTE))
    vt_tiles = peer_v[0].astype(BF16).reshape(N_EXPERTS // TILE_PEER_EXPERTS, TILE_PEER_EXPERTS,
                                               D_MODEL).transpose(0, 2, 1)
    out = _peer(hn2, peer_u[0].astype(BF16), vt_tiles, e1, n1, e2, r2, x1,
                norm_final_w[None, :].astype(F32), _pick(t, TILE_PEER_TOKENS), TILE_PEER_EXPERTS)
    return out.reshape(batch, seq, D_MODEL)
```

```python
import functools
import math

import jax
import jax.numpy as jnp
from jax import lax
from jax.experimental import pallas as pl
from jax.experimental.pallas import tpu as pltpu

F32 = jnp.float32
BF16 = jnp.bfloat16

D_MODEL = 1024
CHUNK = 64
CONV_W = 4
EPS = 1e-6

SSD_HEADS = 8
SSD_HEAD_DIM = 64
SSD_WIDTH = SSD_HEADS * SSD_HEAD_DIM
SSD_GROUPS = 2
SSD_STATE = 128
SSD_BC = SSD_GROUPS * SSD_STATE
SSD_CONV_CH = SSD_WIDTH + 2 * SSD_BC
SSD_GROUP_W = SSD_WIDTH // SSD_GROUPS
SSD_GROUP_HEADS = SSD_HEADS // SSD_GROUPS

GDN_HEADS = 4
GDN_HEAD_K = 128
GDN_HEAD_V = 128
GDN_KEY_WIDTH = GDN_HEADS * GDN_HEAD_K
GDN_WIDTH = GDN_HEADS * GDN_HEAD_V
GDN_CONV_CH = 2 * GDN_KEY_WIDTH + GDN_WIDTH

MIX_WIDTH = SSD_WIDTH + GDN_WIDTH

OFF_SSD_Z = 0
OFF_SSD_XBC = OFF_SSD_Z + SSD_WIDTH
OFF_SSD_DT = OFF_SSD_XBC + SSD_CONV_CH
OFF_GDN_QKV = OFF_SSD_DT + SSD_HEADS
OFF_GDN_GATE = OFF_GDN_QKV + GDN_CONV_CH
OFF_GDN_BETA = OFF_GDN_GATE + GDN_WIDTH
OFF_GDN_ALPHA = OFF_GDN_BETA + GDN_HEADS
IN_COLS = OFF_GDN_ALPHA + GDN_HEADS

PEER_HEADS = 8
PEER_DK = 256
PEER_HALF = PEER_DK // 2
N_KEYS = 128
N_EXPERTS = N_KEYS * N_KEYS
PEER_TOPK = 16

LANES = 128
SUBLANES = 8
SM_DT = 0
SM_BETA = SM_DT + SSD_HEADS
SM_ALPHA = SM_BETA + GDN_HEADS
BIG_Z = 0
BIG_XBC = BIG_Z + SSD_WIDTH
BIG_QKV = BIG_XBC + SSD_CONV_CH
BIG_GATE = BIG_QKV + GDN_CONV_CH
BIG_COLS = BIG_GATE + GDN_WIDTH

CONV_PAD = 8
VMEM_LIMIT = 56 * 1024 * 1024

NT_DIMS = (((1,), (1,)), ((), ()))
TN_DIMS = (((0,), (0,)), ((), ()))


def _softplus(x):
    return jnp.maximum(x, 0.0) + jnp.log1p(jnp.exp(-jnp.abs(x)))


def _silu(x):
    return x * jax.nn.sigmoid(x)


def _bdot(a, b):
    return jnp.dot(a.astype(BF16), b.astype(BF16), preferred_element_type=F32)


def _bdot_nt(a, b):
    return lax.dot_general(a.astype(BF16), b.astype(BF16), NT_DIMS, preferred_element_type=F32)


def _bdot_tn(a, b):
    return lax.dot_general(a.astype(BF16), b.astype(BF16), TN_DIMS, preferred_element_type=F32)


def _split3(x):
    hi = x.astype(BF16)
    r = x - hi.astype(F32)
    mid = r.astype(BF16)
    lo = (r - mid.astype(F32)).astype(BF16)
    return hi, mid, lo


def _sel_dot(x, m01):
    hi, mid, lo = _split3(x)
    d = lambda a: jnp.dot(a, m01, preferred_element_type=F32)
    return d(hi) + d(mid) + d(lo)


def _sel_dot_l(m01, x):
    hi, mid, lo = _split3(x)
    d = lambda a: jnp.dot(m01, a, preferred_element_type=F32)
    return d(hi) + d(mid) + d(lo)


def _inproj_kernel(x_ref, nw_ref, wbig_ref, wsm_ref, wsmt_ref, big_ref, small_ref, smallt_ref):
    x = x_ref[...]
    h = x * lax.rsqrt(jnp.mean(x * x, axis=-1, keepdims=True) + EPS) * nw_ref[...]
    hb = h.astype(BF16)
    big_ref[...] = jnp.dot(hb, wbig_ref[...], preferred_element_type=F32)
    small_ref[...] = jnp.dot(hb, wsm_ref[...], preferred_element_type=F32)
    smallt_ref[...] = lax.dot_general(wsmt_ref[...], hb, NT_DIMS, preferred_element_type=F32)


def _inproj(x2, nw, wbig, wsm, wsmt, tm):
    t = x2.shape[0]
    const = lambda i: (0, 0)
    return pl.pallas_call(
        _inproj_kernel,
        grid=(t // tm,),
        in_specs=[
            pl.BlockSpec((tm, D_MODEL), lambda i: (i, 0)),
            pl.BlockSpec((1, D_MODEL), const),
            pl.BlockSpec((D_MODEL, BIG_COLS), const),
            pl.BlockSpec((D_MODEL, LANES), const),
            pl.BlockSpec((LANES, D_MODEL), const),
        ],
        out_specs=[
            pl.BlockSpec((tm, BIG_COLS), lambda i: (i, 0)),
            pl.BlockSpec((tm, LANES), lambda i: (i, 0)),
            pl.BlockSpec((LANES, tm), lambda i: (0, i)),
        ],
        out_shape=[
            jax.ShapeDtypeStruct((t, BIG_COLS), F32),
            jax.ShapeDtypeStruct((t, LANES), F32),
            jax.ShapeDtypeStruct((LANES, t), F32),
        ],
        compiler_params=pltpu.CompilerParams(
            dimension_semantics=("arbitrary",), vmem_limit_bytes=VMEM_LIMIT),
        name="inproj",
    )(x2, nw, wbig, wsm, wsmt)


def _causal_conv(ext_ref, u, w_ref, rows):
    ext_ref[CONV_PAD:CONV_PAD + rows, :] = u
    base = CONV_PAD - (CONV_W - 1)
    acc = w_ref[0:1, :] * ext_ref[base:base + rows, :]
    for j in range(1, CONV_W):
        acc = acc + w_ref[j:j + 1, :] * ext_ref[base + j:base + j + rows, :]
    ext_ref[base:CONV_PAD, :] = ext_ref[rows + base:rows + CONV_PAD, :]
    return acc


def _unit_lower_inverses(n_mats, eye):
    ps = [eye + n for n in n_mats]
    nks = list(n_mats)
    for _ in range(int(math.log2(CHUNK)) - 1):
        nks = [_bdot(nk, nk) for nk in nks]
        ps = [p + _bdot(p, nk) for p, nk in zip(ps, nks)]
    return ps


def _mixer_kernel(big_ref, small_ref, smallt_ref, x_ref,
                  scw_ref, scb_ref, gcw_ref, prow_ref, pcol_ref,
                  tril_ref, triu_ref, blk_ref, es_ref, eb_ref, eg_ref,
                  drow_ref, snw_ref, gnw_ref, wout_ref, nfw_ref,
                  x1_ref, hn2_ref,
                  ext_s, ext_g, sstate, gstate, ybuf, obuf):
    rows = x_ref.shape[0]
    n_chunks = rows // CHUNK

    @pl.when(pl.program_id(1) == 0)
    def _():
        ext_s[0:CONV_PAD, :] = jnp.zeros((CONV_PAD, SSD_CONV_CH), F32)
        ext_g[0:CONV_PAD, :] = jnp.zeros((CONV_PAD, GDN_CONV_CH), F32)
        sstate[...] = jnp.zeros(sstate.shape, F32)
        gstate[...] = jnp.zeros(gstate.shape, F32)

    sm = small_ref[...]
    sp = _softplus(sm + prow_ref[0:1, :])
    d_a = sp * prow_ref[1:2, :]
    beta = jax.nn.sigmoid(sm)
    smt = smallt_ref[...]
    spt = _softplus(smt + pcol_ref[:, 0:1])
    d_at = spt * pcol_ref[:, 1:2]

    cum = _sel_dot_l(tril_ref[...], d_a)
    tot = _sel_dot_l(blk_ref[...], d_a)
    cumt = _sel_dot(d_at, triu_ref[...])
    e_cum = jnp.exp(cum)
    e_end = jnp.exp(tot - cum)

    xw_scale = _sel_dot(sp * e_end, es_ref[...])
    yoff_scale = _sel_dot(e_cum, es_ref[...])
    beta_x = _sel_dot(beta, eb_ref[...])
    egc_x = _sel_dot(e_cum, eg_ref[...])
    ekd_x = _sel_dot(e_end, eg_ref[...])

    xbc = _silu(_causal_conv(ext_s, big_ref[:, BIG_XBC:BIG_QKV], scw_ref, rows) + scb_ref[...])
    qkv = _silu(_causal_conv(ext_g, big_ref[:, BIG_QKV:BIG_GATE], gcw_ref, rows))

    xs = xbc[:, :SSD_WIDTH]
    xw = xs * xw_scale

    def l2n(t):
        return t * lax.rsqrt(jnp.sum(t * t, axis=-1, keepdims=True) + EPS)

    q_parts, k_parts = [], []
    for hh in range(GDN_HEADS):
        q_parts.append(l2n(qkv[:, hh * GDN_HEAD_K:(hh + 1) * GDN_HEAD_K]))
        k_parts.append(l2n(qkv[:, GDN_KEY_WIDTH + hh * GDN_HEAD_K:GDN_KEY_WIDTH + (hh + 1) * GDN_HEAD_K]))
    q_all = jnp.concatenate(q_parts, axis=1) * (GDN_HEAD_K ** -0.5)
    k_all = jnp.concatenate(k_parts, axis=1)
    v_all = qkv[:, 2 * GDN_KEY_WIDTH:]
    kb = k_all * beta_x
    vb = v_all * beta_x
    kbg = kb * egc_x
    q_dec = q_all * egc_x
    k_dec = k_all * ekd_x

    li = lax.broadcasted_iota(jnp.int32, (CHUNK, CHUNK), 0)
    si = lax.broadcasted_iota(jnp.int32, (CHUNK, CHUNK), 1)
    incl = li >= si
    strict = li > si
    eye = jnp.where(li == si, 1.0, 0.0).astype(F32)
    neg_inf = jnp.float32(-jnp.inf)

    def decay(col, r0):
        seg = cum[r0:r0 + CHUNK, col:col + 1] - cumt[col:col + 1, r0:r0 + CHUNK]
        return jnp.exp(jnp.where(incl, seg, neg_inf))

    chunks = [(c, c * CHUNK) for c in range(n_chunks)]

    b_ms, c_ms = {}, {}
    for c, r0 in chunks:
        for g in range(SSD_GROUPS):
            b_ms[c, g] = xbc[r0:r0 + CHUNK, SSD_WIDTH + g * SSD_STATE:SSD_WIDTH + (g + 1) * SSD_STATE]
            c_ms[c, g] = xbc[r0:r0 + CHUNK,
                             SSD_WIDTH + SSD_BC + g * SSD_STATE:SSD_WIDTH + SSD_BC + (g + 1) * SSD_STATE]
    cbs = {k: _bdot_nt(c_ms[k], b_ms[k]) for k in b_ms}
    for c, r0 in chunks:
        for hh in range(SSD_HEADS):
            col = SM_DT + hh
            m_h = cbs[c, hh // SSD_GROUP_HEADS] * decay(col, r0) * spt[col:col + 1, r0:r0 + CHUNK]
            h0 = hh * SSD_HEAD_DIM
            ybuf[r0:r0 + CHUNK, h0:h0 + SSD_HEAD_DIM] = _bdot(m_h, xs[r0:r0 + CHUNK, h0:h0 + SSD_HEAD_DIM])

    decs, kqs = {}, {}
    for c, r0 in chunks:
        for hh in range(GDN_HEADS):
            k0 = hh * GDN_HEAD_K
            decs[c, hh] = decay(SM_ALPHA + hh, r0)
            lhs = jnp.concatenate([kb[r0:r0 + CHUNK, k0:k0 + GDN_HEAD_K],
                                   q_all[r0:r0 + CHUNK, k0:k0 + GDN_HEAD_K]], axis=0)
            kqs[c, hh] = _bdot_nt(lhs, k_all[r0:r0 + CHUNK, k0:k0 + GDN_HEAD_K])
    keys = list(decs)
    n_mats = [jnp.where(strict, -(kqs[k][:CHUNK] * decs[k]), 0.0) for k in keys]
    t_mats = dict(zip(keys, _unit_lower_inverses(n_mats, eye)))
    uws = {}
    for c, r0 in chunks:
        for hh in range(GDN_HEADS):
            k0 = hh * GDN_HEAD_K
            rhs = jnp.concatenate([vb[r0:r0 + CHUNK, k0:k0 + GDN_HEAD_K],
                                   kbg[r0:r0 + CHUNK, k0:k0 + GDN_HEAD_K]], axis=1)
            uws[c, hh] = _bdot(t_mats[c, hh], rhs)

    for c, r0 in chunks:
        r1 = r0 + CHUNK
        for g in range(SSD_GROUPS):
            c0 = g * SSD_GROUP_W
            st = sstate[g]
            y_off = _bdot(c_ms[c, g], st) * yoff_scale[r0:r1, c0:c0 + SSD_GROUP_W]
            ybuf[r0:r1, c0:c0 + SSD_GROUP_W] = (ybuf[r0:r1, c0:c0 + SSD_GROUP_W] + y_off
                                                + drow_ref[:, c0:c0 + SSD_GROUP_W] * xs[r0:r1, c0:c0 + SSD_GROUP_W])
            s_new = _bdot_tn(b_ms[c, g], xw[r0:r1, c0:c0 + SSD_GROUP_W])
            sstate[g] = st * yoff_scale[r1 - 1:r1, c0:c0 + SSD_GROUP_W] + s_new
        for hh in range(GDN_HEADS):
            k0 = hh * GDN_HEAD_K
            k1 = k0 + GDN_HEAD_K
            state = gstate[hh]
            lhs = jnp.concatenate([uws[c, hh][:, GDN_HEAD_V:], q_dec[r0:r1, k0:k1]], axis=0)
            ws_qs = _bdot(lhs, state)
            v_new = uws[c, hh][:, :GDN_HEAD_V] - ws_qs[:CHUNK]
            qk = kqs[c, hh][CHUNK:] * decs[c, hh]
            obuf[r0:r1, k0:k1] = ws_qs[CHUNK:] + _bdot(qk, v_new)
            gstate[hh] = state * egc_x[r1 - 1:r1, k0:k1] + _bdot_tn(k_dec[r0:r1, k0:k1], v_new)

    z = big_ref[:, BIG_Z:BIG_XBC]
    t = ybuf[...] * _silu(z)
    y_ssd = t * lax.rsqrt(jnp.mean(t * t, axis=-1, keepdims=True) + EPS) * snw_ref[...]
    o = obuf[...]
    gate = big_ref[:, BIG_GATE:BIG_COLS]
    parts = [y_ssd]
    for hh in range(GDN_HEADS):
        k0 = hh * GDN_HEAD_V
        o_h = o[:, k0:k0 + GDN_HEAD_V]
        parts.append(o_h * lax.rsqrt(jnp.mean(o_h * o_h, axis=-1, keepdims=True) + EPS)
                     * gnw_ref[...] * _silu(gate[:, k0:k0 + GDN_HEAD_V]))
    mixed = jnp.concatenate(parts, axis=1).astype(BF16)
    x1 = x_ref[...] + jnp.dot(mixed, wout_ref[...], preferred_element_type=F32)
    x1_ref[...] = x1
    hn2 = x1 * lax.rsqrt(jnp.mean(x1 * x1, axis=-1, keepdims=True) + EPS) * nfw_ref[...]
    hn2_ref[...] = hn2.astype(BF16)


def _mixer(big, small, smallt, x2, consts, batch, seq, rows):
    t = x2.shape[0]
    nblk = seq // rows
    tok = lambda b, s: (b * nblk + s, 0)
    tokt = lambda b, s: (0, b * nblk + s)
    const = lambda b, s: (0, 0)
    return pl.pallas_call(
        _mixer_kernel,
        grid=(batch, nblk),
        in_specs=[
            pl.BlockSpec((rows, BIG_COLS), tok),
            pl.BlockSpec((rows, LANES), tok),
            pl.BlockSpec((LANES, rows), tokt),
            pl.BlockSpec((rows, D_MODEL), tok),
        ] + [pl.BlockSpec(a.shape, const) for a in consts],
        out_specs=[
            pl.BlockSpec((rows, D_MODEL), tok),
            pl.BlockSpec((rows, D_MODEL), tok),
        ],
        out_shape=[
            jax.ShapeDtypeStruct((t, D_MODEL), F32),
            jax.ShapeDtypeStruct((t, D_MODEL), BF16),
        ],
        scratch_shapes=[
            pltpu.VMEM((rows + CONV_PAD, SSD_CONV_CH), F32),
            pltpu.VMEM((rows + CONV_PAD, GDN_CONV_CH), F32),
            pltpu.VMEM((SSD_GROUPS, SSD_STATE, SSD_GROUP_W), F32),
            pltpu.VMEM((GDN_HEADS, GDN_HEAD_K, GDN_HEAD_V), F32),
            pltpu.VMEM((rows, SSD_WIDTH), F32),
            pltpu.VMEM((rows, GDN_WIDTH), F32),
        ],
        compiler_params=pltpu.CompilerParams(
            dimension_semantics=("arbitrary", "arbitrary"), vmem_limit_bytes=VMEM_LIMIT),
        name="mixer",
    )(big, small, smallt, x2, *consts)


def _batcher_network(n):
    def merge(lo, hi, r):
        step = r * 2
        if step < hi - lo:
            yield from merge(lo, hi, step)
            yield from merge(lo + r, hi, step)
            yield from [(i, i + r) for i in range(lo + r, hi - r, step)]
        else:
            yield (lo, lo + r)

    def sort(lo, hi):
        if hi - lo >= 1:
            mid = lo + (hi - lo) // 2
            yield from sort(lo, mid)
            yield from sort(mid + 1, hi)
            yield from merge(lo, hi, 1)

    return list(sort(0, n - 1))


def _top_desc_tiles(x, count):
    n_tiles = x.shape[0] // SUBLANES
    v = [x[k * SUBLANES:(k + 1) * SUBLANES] for k in range(n_tiles)]
    for i, j in _batcher_network(1 << (n_tiles - 1).bit_length()):
        if j < n_tiles:
            v[i], v[j] = jnp.maximum(v[i], v[j]), jnp.minimum(v[i], v[j])
    exhausted = jnp.float32(-jnp.inf)
    vals = []
    for r in range(count):
        m = jnp.max(v[0], axis=0, keepdims=True)
        vals.append(m)
        hit = v[0] == m
        for k in range(min(count - r - 1, n_tiles)):
            v[k] = jnp.where(hit, v[k + 1] if k + 1 < n_tiles else exhausted, v[k])
    return jnp.concatenate(vals, axis=0)


def _rank_products(a, b, sub8):
    cands = [b * a[0:1], b[0:8] * a[1:2]]
    for i in range(2, 8):
        cands.append(jnp.where(sub8 < (PEER_TOPK // (i + 1)), b[0:8] * a[i:i + 1], 0.0))
    cands.append(a[8:16] * b[0:1])
    return jnp.concatenate(cands, axis=0)


def _route_kernel(hn2_ref, wqt_ref, keys_ref, e1_ref, n1_ref, e2_ref, r2_ref, qt_scr):
    tn = hn2_ref.shape[0]
    qt_scr[...] = lax.dot_general(wqt_ref[...], hn2_ref[...], NT_DIMS, preferred_element_type=F32)
    sub8 = lax.broadcasted_iota(jnp.int32, (SUBLANES, LANES), 0)

    def head_body(h, carry):
        q0 = pl.multiple_of(h * PEER_DK, PEER_DK)
        s1_all = _bdot(keys_ref[0], qt_scr[pl.ds(q0, PEER_HALF), :])
        s2_all = _bdot(keys_ref[1], qt_scr[pl.ds(q0 + PEER_HALF, PEER_HALF), :])
        for lg in range(tn // LANES):
            l0 = lg * LANES
            s1 = s1_all[:, l0:l0 + LANES]
            s2 = s2_all[:, l0:l0 + LANES]
            x1 = jnp.exp(s1 - jnp.max(s1, axis=0, keepdims=True))
            x2 = jnp.exp(s2 - jnp.max(s2, axis=0, keepdims=True))
            a = _top_desc_tiles(x1, PEER_TOPK)
            b = _top_desc_tiles(x2, PEER_TOPK)
            cand = _rank_products(a, b, sub8)
            theta = _top_desc_tiles(cand, PEER_TOPK)[PEER_TOPK - 1:PEER_TOPK]
            chosen = cand >= theta
            zsum = jnp.sum(jnp.where(chosen, cand, 0.0), axis=0, keepdims=True)
            rz = 0.5 / zsum
            picked = jnp.where(chosen, 1.0, 0.0)
            counts = [jnp.sum(picked[0:PEER_TOPK], axis=0, keepdims=True)]
            for i in range(1, 8):
                r0 = PEER_TOPK + 8 * (i - 1)
                counts.append(jnp.sum(picked[r0:r0 + 8], axis=0, keepdims=True))
            n_rank = jnp.concatenate(counts + [picked[PEER_TOPK + 56:PEER_TOPK + 64]], axis=0)
            n1 = jnp.zeros_like(x1)
            r2 = jnp.full_like(x2, float(PEER_TOPK))
            for i in range(PEER_TOPK):
                n1 = jnp.where(x1 == a[i:i + 1], n_rank[i:i + 1], n1)
                r2 = jnp.where(x2 == b[i:i + 1], float(i), r2)
            e1_ref[h, :, l0:l0 + LANES] = x1 * rz
            n1_ref[h, :, l0:l0 + LANES] = n1
            e2_ref[h, :, l0:l0 + LANES] = x2.astype(BF16)
            r2_ref[h, :, l0:l0 + LANES] = r2.astype(BF16)
        return carry

    lax.fori_loop(0, PEER_HEADS, head_body, 0)


def _route(hn2, wqt, keys, tn):
    t = hn2.shape[0]
    sc_spec = pl.BlockSpec((PEER_HEADS, N_KEYS, tn), lambda i: (0, 0, i))
    sc_shape = jax.ShapeDtypeStruct((PEER_HEADS, N_KEYS, t), F32)
    sc_shape_bf = jax.ShapeDtypeStruct((PEER_HEADS, N_KEYS, t), BF16)
    return pl.pallas_call(
        _route_kernel,
        grid=(t // tn,),
        in_specs=[
            pl.BlockSpec((tn, D_MODEL), lambda i: (i, 0)),
            pl.BlockSpec(wqt.shape, lambda i: (0, 0)),
            pl.BlockSpec(keys.shape, lambda i: (0, 0, 0)),
        ],
        out_specs=[sc_spec, sc_spec, sc_spec, sc_spec],
        out_shape=[sc_shape, sc_shape, sc_shape_bf, sc_shape_bf],
        scratch_shapes=[pltpu.VMEM((PEER_HEADS * PEER_DK, tn), F32)],
        compiler_params=pltpu.CompilerParams(
            dimension_semantics=("arbitrary",), vmem_limit_bytes=VMEM_LIMIT),
        name="route",
    )(hn2, wqt, keys)


PEER_ROWS = 16


PEER_LANES = 256
MXU_N = 256
PEER_PIECE_ROWS = 512


def _peer_step(gate_tile, hn2t_scr, u_ref, vt_ref, e1_ref, n1_ref, e2_scr, r2_scr, yt_scr,
               e1b_scr, n1b_scr,
               act_write, act_read, w_write, w_read):
    te = u_ref.shape[0]
    tm = hn2t_scr.shape[1]
    n_e1 = te // N_KEYS

    def first_matmul(m, n):
        rows = slice(m * PEER_PIECE_ROWS, (m + 1) * PEER_PIECE_ROWS)
        act_write[rows, n * MXU_N:(n + 1) * MXU_N] = jnp.dot(
            u_ref[rows, :], hn2t_scr[:, n * MXU_N:(n + 1) * MXU_N], preferred_element_type=F32)

    def second_matmul(m, n):
        rows = slice(m * PEER_PIECE_ROWS, (m + 1) * PEER_PIECE_ROWS)
        yt_scr[rows, n * MXU_N:(n + 1) * MXU_N] += jnp.dot(
            vt_ref[rows, :], w_read[:, n * MXU_N:(n + 1) * MXU_N], preferred_element_type=F32)

    for e1l in range(n_e1):
        for h in range(PEER_HEADS):
            row = pl.ds(gate_tile * n_e1 + e1l, 1)
            e1b_scr[e1l, h] = jnp.broadcast_to(e1_ref[h, row, :], (PEER_ROWS, tm)).astype(BF16)
            n1b_scr[e1l, h] = jnp.broadcast_to(n1_ref[h, row, :], (PEER_ROWS, tm)).astype(BF16)

    def gates(e1l, lb):
        n_groups = N_KEYS // PEER_ROWS
        lanes = slice(lb * PEER_LANES, (lb + 1) * PEER_LANES)
        accs = [None] * n_groups
        zero = jnp.zeros((PEER_ROWS, PEER_LANES), BF16)
        for h in range(PEER_HEADS):
            e1b = e1b_scr[e1l, h, :, lanes]
            n1b = n1b_scr[e1l, h, :, lanes]
            for r in range(n_groups):
                rows = slice(r * PEER_ROWS, (r + 1) * PEER_ROWS)
                sel = jnp.where(r2_scr[h, rows, lanes] < n1b, e2_scr[h, rows, lanes] * e1b, zero)
                accs[r] = sel if accs[r] is None else accs[r] + sel
        for r in range(n_groups):
            rows = slice(e1l * N_KEYS + r * PEER_ROWS, e1l * N_KEYS + (r + 1) * PEER_ROWS)
            a = act_read[rows, lanes]
            w_write[rows, lanes] = accs[r] * (a * (1.0 + lax.erf(a * (2.0 ** -0.5)))).astype(BF16)

    pieces = ([functools.partial(first_matmul, m, n)
               for n in range(tm // MXU_N) for m in range(te // PEER_PIECE_ROWS)]
              + [functools.partial(second_matmul, m, n)
                 for n in range(tm // MXU_N) for m in range(D_MODEL // PEER_PIECE_ROWS)])
    blocks = [(e1l, lb) for e1l in range(n_e1) for lb in range(tm // PEER_LANES)]
    per_block = -(-len(pieces) // len(blocks))
    for i, (e1l, lb) in enumerate(blocks):
        for piece in pieces[i * per_block:(i + 1) * per_block]:
            piece()
        gates(e1l, lb)
    for piece in pieces[len(blocks) * per_block:]:
        piece()


def _peer_kernel(hn2_ref, u_ref, vt_ref, e1_ref, n1_ref, e2_ref, r2_ref, x1_ref, nw_ref,
                 out_ref, yt_scr, act0_scr, act1_scr, w0_scr, w1_scr, e1b_scr, n1b_scr, e2_scr, r2_scr,
                 hn2t_scr,
                 *, n_tiles, n_steps):
    s = pl.program_id(0)
    tm = hn2_ref.shape[0]
    gate_tile = jnp.clip(s - 1, 0, n_steps - 1) % n_tiles
    lag_tile = jnp.clip(s - 2, 0, n_steps - 1) % n_tiles

    @pl.when(s == 0)
    def _():
        act1_scr[...] = jnp.zeros(act1_scr.shape, F32)
        w1_scr[...] = jnp.zeros(w1_scr.shape, BF16)

    @pl.when(lag_tile == 0)
    def _():
        yt_scr[...] = jnp.zeros(yt_scr.shape, F32)

    @pl.when(gate_tile == 0)
    def _():
        for h in range(PEER_HEADS):
            e2_scr[h] = e2_ref[h]
            r2_scr[h] = r2_ref[h]

    @pl.when(jnp.minimum(s, n_steps - 1) % n_tiles == 0)
    def _():
        hn2t_scr[...] = hn2_ref[...].astype(F32).T.astype(BF16)

    args = (gate_tile, hn2t_scr, u_ref, vt_ref, e1_ref, n1_ref, e2_scr, r2_scr, yt_scr,
            e1b_scr, n1b_scr)

    @pl.when(s % 2 == 0)
    def _():
        _peer_step(*args, act0_scr, act1_scr, w0_scr, w1_scr)

    @pl.when(s % 2 == 1)
    def _():
        _peer_step(*args, act1_scr, act0_scr, w1_scr, w0_scr)

    @pl.when((lag_tile == n_tiles - 1) & (s >= 2))
    def _():
        x = x1_ref[...] + yt_scr[...].T
        out_ref[...] = x * lax.rsqrt(jnp.mean(x * x, axis=-1, keepdims=True) + EPS) * nw_ref[...]


def _peer(hn2, u_bf, vt_bf, e1, n1, e2, r2, x1, nw, tm, te):
    t = hn2.shape[0]
    n_tiles = N_EXPERTS // te
    n_steps = (t // tm) * n_tiles
    assert tm % MXU_N == 0 and tm % PEER_LANES == 0 and te % PEER_PIECE_ROWS == 0
    tile_a = lambda s: jnp.minimum(s, n_steps - 1)
    tile_b = lambda s: jnp.clip(s - 1, 0, n_steps - 1)
    tile_c = lambda s: jnp.clip(s - 2, 0, n_steps - 1)
    sc_spec = pl.BlockSpec((PEER_HEADS, N_KEYS, tm), lambda s: (0, 0, tile_b(s) // n_tiles))
    return pl.pallas_call(
        functools.partial(_peer_kernel, n_tiles=n_tiles, n_steps=n_steps),
        grid=(n_steps + 2,),
        in_specs=[
            pl.BlockSpec((tm, D_MODEL), lambda s: (tile_a(s) // n_tiles, 0)),
            pl.BlockSpec((te, D_MODEL), lambda s: (tile_a(s) % n_tiles, 0)),
            pl.BlockSpec((None, D_MODEL, te), lambda s: (tile_c(s) % n_tiles, 0, 0)),
            sc_spec, sc_spec, sc_spec, sc_spec,
            pl.BlockSpec((tm, D_MODEL), lambda s: (tile_c(s) // n_tiles, 0)),
            pl.BlockSpec((1, D_MODEL), lambda s: (0, 0)),
        ],
        out_specs=pl.BlockSpec((tm, D_MODEL), lambda s: (tile_c(s) // n_tiles, 0)),
        out_shape=jax.ShapeDtypeStruct((t, D_MODEL), F32),
        scratch_shapes=[
            pltpu.VMEM((D_MODEL, tm), F32),
            pltpu.VMEM((te, tm), F32),
            pltpu.VMEM((te, tm), F32),
            pltpu.VMEM((te, tm), BF16),
            pltpu.VMEM((te, tm), BF16),
            pltpu.VMEM((te // N_KEYS, PEER_HEADS, PEER_ROWS, tm), BF16),
            pltpu.VMEM((te // N_KEYS, PEER_HEADS, PEER_ROWS, tm), BF16),
            pltpu.VMEM((PEER_HEADS, N_KEYS, tm), BF16),
            pltpu.VMEM((PEER_HEADS, N_KEYS, tm), BF16),
            pltpu.VMEM((D_MODEL, tm), BF16),
        ],
        compiler_params=pltpu.CompilerParams(
            dimension_semantics=("arbitrary",), vmem_limit_bytes=VMEM_LIMIT),
        name="peer",
    )(hn2, u_bf, vt_bf, e1, n1, e2, r2, x1, nw)


def _pick(n, pref):
    t = min(pref, n)
    while n % t:
        t //= 2
    return t


def _expansion(src_col0, n_heads, width):
    r = jnp.arange(LANES)[:, None]
    c = jnp.arange(n_heads * width)[None, :]
    return (r == src_col0 + c // width).astype(BF16)


def _chunk_masks(rows):
    r = jnp.arange(rows)[:, None]
    c = jnp.arange(rows)[None, :]
    same = (r // CHUNK) == (c // CHUNK)
    return ((same & (r >= c)).astype(BF16), (same & (r <= c)).astype(BF16), same.astype(BF16))


TILE_INPROJ = 512
TILE_MIXER = 256
TILE_ROUTE = 512
TILE_PEER_TOKENS = 512
TILE_PEER_EXPERTS = 1024


def kernel(x, norm_mix_w, w_in, ssd_conv_w, ssd_conv_b, ssd_dt_bias, ssd_a_log, ssd_d, ssd_norm_w,
           gdn_conv_w, gdn_dt_bias, gdn_a_log, gdn_norm_w, w_out, norm_ffn_w, peer_w_q, peer_sub_keys,
           peer_u, peer_v, norm_final_w):
    batch, seq, _ = x.shape
    assert w_in.shape[0] == 1, "single trunk layer"
    t = batch * seq
    x2 = x.reshape(t, D_MODEL)
    w_in = w_in[0]

    wbig = jnp.concatenate([w_in[:, OFF_SSD_Z:OFF_SSD_DT], w_in[:, OFF_GDN_QKV:OFF_GDN_BETA]],
                           axis=1).astype(BF16)
    wsm = jnp.concatenate([w_in[:, OFF_SSD_DT:OFF_GDN_QKV], w_in[:, OFF_GDN_BETA:IN_COLS]], axis=1)
    wsm = jnp.pad(wsm, ((0, 0), (0, LANES - wsm.shape[1]))).astype(BF16)
    zeros4 = jnp.zeros((GDN_HEADS,), F32)
    bias = jnp.concatenate([ssd_dt_bias[0].astype(F32), zeros4, gdn_dt_bias[0].astype(F32)])
    amul = jnp.concatenate([-jnp.exp(ssd_a_log[0].astype(F32)), zeros4, -jnp.exp(gdn_a_log[0].astype(F32))])
    pad = LANES - bias.shape[0]
    prow = jnp.pad(jnp.stack([bias, amul]), ((0, SUBLANES - 2), (0, pad)))
    pcol = jnp.pad(jnp.stack([bias, amul], axis=1), ((0, pad), (0, LANES - 2)))
    rows = _pick(seq, TILE_MIXER)
    tril, triu, blk = _chunk_masks(rows)
    consts = [
        ssd_conv_w[0].astype(F32), ssd_conv_b[0][None, :].astype(F32), gdn_conv_w[0].astype(F32),
        prow, pcol, tril, triu, blk,
        _expansion(SM_DT, SSD_HEADS, SSD_HEAD_DIM),
        _expansion(SM_BETA, GDN_HEADS, GDN_HEAD_V),
        _expansion(SM_ALPHA, GDN_HEADS, GDN_HEAD_V),
        jnp.repeat(ssd_d[0].astype(F32), SSD_HEAD_DIM)[None, :],
        ssd_norm_w[0][None, :].astype(F32), gdn_norm_w[0][None, :].astype(F32),
        w_out[0].astype(BF16), norm_ffn_w[0][None, :].astype(F32),
    ]

    big, small, smallt = _inproj(x2, norm_mix_w[0][None, :].astype(F32), wbig, wsm, wsm.T,
                                 _pick(t, TILE_INPROJ))
    x1, hn2 = _mixer(big, small, smallt, x2, consts, batch, seq, rows)
    e1, n1, e2, r2 = _route(hn2, peer_w_q[0].T.astype(BF16), peer_sub_keys[0].astype(BF16),
                        _pick(t, TILE_ROUTE))
    vt_tiles = peer_v[0].astype(BF16).reshape(N_EXPERTS // TILE_PEER_EXPERTS, TILE_PEER_EXPERTS,
                                               D_MODEL).transpose(0, 2, 1)
    out = _peer(hn2, peer_u[0].astype(BF16), vt_tiles, e1, n1, e2, r2, x1,
                norm_final_w[None, :].astype(F32), _pick(t, TILE_PEER_TOKENS), TILE_PEER_EXPERTS)
    return out.reshape(batch, seq, D_MODEL)
```

```python
import functools
import math

import jax
import jax.numpy as jnp
from jax import lax
from jax.experimental import pallas as pl
from jax.experimental.pallas import tpu as pltpu

F32 = jnp.float32
BF16 = jnp.bfloat16

D_MODEL = 1024
CHUNK = 64
CONV_W = 4
EPS = 1e-6

SSD_HEADS = 8
SSD_HEAD_DIM = 64
SSD_WIDTH = SSD_HEADS * SSD_HEAD_DIM
SSD_GROUPS = 2
SSD_STATE = 128
SSD_BC = SSD_GROUPS * SSD_STATE
SSD_CONV_CH = SSD_WIDTH + 2 * SSD_BC
SSD_GROUP_W = SSD_WIDTH // SSD_GROUPS
SSD_GROUP_HEADS = SSD_HEADS // SSD_GROUPS

GDN_HEADS = 4
GDN_HEAD_K = 128
GDN_HEAD_V = 128
GDN_KEY_WIDTH = GDN_HEADS * GDN_HEAD_K
GDN_WIDTH = GDN_HEADS * GDN_HEAD_V
GDN_CONV_CH = 2 * GDN_KEY_WIDTH + GDN_WIDTH

MIX_WIDTH = SSD_WIDTH + GDN_WIDTH

OFF_SSD_Z = 0
OFF_SSD_XBC = OFF_SSD_Z + SSD_WIDTH
OFF_SSD_DT = OFF_SSD_XBC + SSD_CONV_CH
OFF_GDN_QKV = OFF_SSD_DT + SSD_HEADS
OFF_GDN_GATE = OFF_GDN_QKV + GDN_CONV_CH
OFF_GDN_BETA = OFF_GDN_GATE + GDN_WIDTH
OFF_GDN_ALPHA = OFF_GDN_BETA + GDN_HEADS
IN_COLS = OFF_GDN_ALPHA + GDN_HEADS

PEER_HEADS = 8
PEER_DK = 256
PEER_HALF = PEER_DK // 2
N_KEYS = 128
N_EXPERTS = N_KEYS * N_KEYS
PEER_TOPK = 16

LANES = 128
SUBLANES = 8
SM_DT = 0
SM_BETA = SM_DT + SSD_HEADS
SM_ALPHA = SM_BETA + GDN_HEADS
BIG_Z = 0
BIG_XBC = BIG_Z + SSD_WIDTH
BIG_QKV = BIG_XBC + SSD_CONV_CH
BIG_GATE = BIG_QKV + GDN_CONV_CH
BIG_COLS = BIG_GATE + GDN_WIDTH

CONV_PAD = 8
VMEM_LIMIT = 56 * 1024 * 1024

NT_DIMS = (((1,), (1,)), ((), ()))
TN_DIMS = (((0,), (0,)), ((), ()))


def _softplus(x):
    return jnp.maximum(x, 0.0) + jnp.log1p(jnp.exp(-jnp.abs(x)))


def _silu(x):
    return x * jax.nn.sigmoid(x)


def _bdot(a, b):
    return jnp.dot(a.astype(BF16), b.astype(BF16), preferred_element_type=F32)


def _bdot_nt(a, b):
    return lax.dot_general(a.astype(BF16), b.astype(BF16), NT_DIMS, preferred_element_type=F32)


def _bdot_tn(a, b):
    return lax.dot_general(a.astype(BF16), b.astype(BF16), TN_DIMS, preferred_element_type=F32)


def _split3(x):
    hi = x.astype(BF16)
    r = x - hi.astype(F32)
    mid = r.astype(BF16)
    lo = (r - mid.astype(F32)).astype(BF16)
    return hi, mid, lo


def _sel_dot(x, m01):
    hi, mid, lo = _split3(x)
    d = lambda a: jnp.dot(a, m01, preferred_element_type=F32)
    return d(hi) + d(mid) + d(lo)


def _sel_dot_l(m01, x):
    hi, mid, lo = _split3(x)
    d = lambda a: jnp.dot(m01, a, preferred_element_type=F32)
    return d(hi) + d(mid) + d(lo)


def _causal_conv(ext_ref, u, w_ref, rows):
    ext_ref[CONV_PAD:CONV_PAD + rows, :] = u
    base = CONV_PAD - (CONV_W - 1)
    acc = w_ref[0:1, :] * ext_ref[base:base + rows, :]
    for j in range(1, CONV_W):
        acc = acc + w_ref[j:j + 1, :] * ext_ref[base + j:base + j + rows, :]
    ext_ref[base:CONV_PAD, :] = ext_ref[rows + base:rows + CONV_PAD, :]
    return acc


def _unit_lower_inverses(n_mats, eye):
    ps = [eye + n for n in n_mats]
    nks = list(n_mats)
    for _ in range(int(math.log2(CHUNK)) - 1):
        nks = [_bdot(nk, nk) for nk in nks]
        ps = [p + _bdot(p, nk) for p, nk in zip(ps, nks)]
    return ps


def _mixer_kernel(x_ref, nmw_ref, wbig_ref, wsm_ref, wsmt_ref,
                  scw_ref, scb_ref, gcw_ref, prow_ref, pcol_ref,
                  tril_ref, triu_ref, blk_ref, es_ref, eb_ref, eg_ref,
                  drow_ref, snw_ref, gnw_ref, wout_ref, nfw_ref,
                  x1_ref, hn2_ref,
                  ext_s, ext_g, sstate, gstate, ybuf, obuf, big_ref):
    rows = x_ref.shape[0]
    n_chunks = rows // CHUNK

    xin = x_ref[...]
    hb = (xin * lax.rsqrt(jnp.mean(xin * xin, axis=-1, keepdims=True) + EPS) * nmw_ref[...]).astype(BF16)
    big_ref[...] = jnp.dot(hb, wbig_ref[...], preferred_element_type=F32)
    small_proj = jnp.dot(hb, wsm_ref[...], preferred_element_type=F32)
    smallt_proj = lax.dot_general(wsmt_ref[...], hb, NT_DIMS, preferred_element_type=F32)

    @pl.when(pl.program_id(1) == 0)
    def _():
        ext_s[0:CONV_PAD, :] = jnp.zeros((CONV_PAD, SSD_CONV_CH), F32)
        ext_g[0:CONV_PAD, :] = jnp.zeros((CONV_PAD, GDN_CONV_CH), F32)
        sstate[...] = jnp.zeros(sstate.shape, F32)
        gstate[...] = jnp.zeros(gstate.shape, F32)

    sm = small_proj
    sp = _softplus(sm + prow_ref[0:1, :])
    d_a = sp * prow_ref[1:2, :]
    beta = jax.nn.sigmoid(sm)
    smt = smallt_proj
    spt = _softplus(smt + pcol_ref[:, 0:1])
    d_at = spt * pcol_ref[:, 1:2]

    cum = _sel_dot_l(tril_ref[...], d_a)
    tot = _sel_dot_l(blk_ref[...], d_a)
    cumt = _sel_dot(d_at, triu_ref[...])
    e_cum = jnp.exp(cum)
    e_end = jnp.exp(tot - cum)

    xw_scale = _sel_dot(sp * e_end, es_ref[...])
    yoff_scale = _sel_dot(e_cum, es_ref[...])
    beta_x = _sel_dot(beta, eb_ref[...])
    egc_x = _sel_dot(e_cum, eg_ref[...])
    ekd_x = _sel_dot(e_end, eg_ref[...])

    xbc = _silu(_causal_conv(ext_s, big_ref[:, BIG_XBC:BIG_QKV], scw_ref, rows) + scb_ref[...])
    qkv = _silu(_causal_conv(ext_g, big_ref[:, BIG_QKV:BIG_GATE], gcw_ref, rows))

    xs = xbc[:, :SSD_WIDTH]
    xw = xs * xw_scale

    def l2n(t):
        return t * lax.rsqrt(jnp.sum(t * t, axis=-1, keepdims=True) + EPS)

    q_parts, k_parts = [], []
    for hh in range(GDN_HEADS):
        q_parts.append(l2n(qkv[:, hh * GDN_HEAD_K:(hh + 1) * GDN_HEAD_K]))
        k_parts.append(l2n(qkv[:, GDN_KEY_WIDTH + hh * GDN_HEAD_K:GDN_KEY_WIDTH + (hh + 1) * GDN_HEAD_K]))
    q_all = jnp.concatenate(q_parts, axis=1) * (GDN_HEAD_K ** -0.5)
    k_all = jnp.concatenate(k_parts, axis=1)
    v_all = qkv[:, 2 * GDN_KEY_WIDTH:]
    kb = k_all * beta_x
    vb = v_all * beta_x
    kbg = kb * egc_x
    q_dec = q_all * egc_x
    k_dec = k_all * ekd_x

    li = lax.broadcasted_iota(jnp.int32, (CHUNK, CHUNK), 0)
    si = lax.broadcasted_iota(jnp.int32, (CHUNK, CHUNK), 1)
    incl = li >= si
    strict = li > si
    eye = jnp.where(li == si, 1.0, 0.0).astype(F32)
    neg_inf = jnp.float32(-jnp.inf)

    def decay(col, r0):
        seg = cum[r0:r0 + CHUNK, col:col + 1] - cumt[col:col + 1, r0:r0 + CHUNK]
        return jnp.exp(jnp.where(incl, seg, neg_inf))

    chunks = [(c, c * CHUNK) for c in range(n_chunks)]

    b_ms, c_ms = {}, {}
    for c, r0 in chunks:
        for g in range(SSD_GROUPS):
            b_ms[c, g] = xbc[r0:r0 + CHUNK, SSD_WIDTH + g * SSD_STATE:SSD_WIDTH + (g + 1) * SSD_STATE]
            c_ms[c, g] = xbc[r0:r0 + CHUNK,
                             SSD_WIDTH + SSD_BC + g * SSD_STATE:SSD_WIDTH + SSD_BC + (g + 1) * SSD_STATE]
    cbs = {k: _bdot_nt(c_ms[k], b_ms[k]) for k in b_ms}
    for c, r0 in chunks:
        for hh in range(SSD_HEADS):
            col = SM_DT + hh
            m_h = cbs[c, hh // SSD_GROUP_HEADS] * decay(col, r0) * spt[col:col + 1, r0:r0 + CHUNK]
            h0 = hh * SSD_HEAD_DIM
            ybuf[r0:r0 + CHUNK, h0:h0 + SSD_HEAD_DIM] = _bdot(m_h, xs[r0:r0 + CHUNK, h0:h0 + SSD_HEAD_DIM])

    decs, kqs = {}, {}
    for c, r0 in chunks:
        for hh in range(GDN_HEADS):
            k0 = hh * GDN_HEAD_K
            decs[c, hh] = decay(SM_ALPHA + hh, r0)
            lhs = jnp.concatenate([kb[r0:r0 + CHUNK, k0:k0 + GDN_HEAD_K],
                                   q_all[r0:r0 + CHUNK, k0:k0 + GDN_HEAD_K]], axis=0)
            kqs[c, hh] = _bdot_nt(lhs, k_all[r0:r0 + CHUNK, k0:k0 + GDN_HEAD_K])
    keys = list(decs)
    n_mats = [jnp.where(strict, -(kqs[k][:CHUNK] * decs[k]), 0.0) for k in keys]
    t_mats = dict(zip(keys, _unit_lower_inverses(n_mats, eye)))
    uws = {}
    for c, r0 in chunks:
        for hh in range(GDN_HEADS):
            k0 = hh * GDN_HEAD_K
            rhs = jnp.concatenate([vb[r0:r0 + CHUNK, k0:k0 + GDN_HEAD_K],
                                   kbg[r0:r0 + CHUNK, k0:k0 + GDN_HEAD_K]], axis=1)
            uws[c, hh] = _bdot(t_mats[c, hh], rhs)

    for c, r0 in chunks:
        r1 = r0 + CHUNK
        for g in range(SSD_GROUPS):
            c0 = g * SSD_GROUP_W
            st = sstate[g]
            y_off = _bdot(c_ms[c, g], st) * yoff_scale[r0:r1, c0:c0 + SSD_GROUP_W]
            ybuf[r0:r1, c0:c0 + SSD_GROUP_W] = (ybuf[r0:r1, c0:c0 + SSD_GROUP_W] + y_off
                                                + drow_ref[:, c0:c0 + SSD_GROUP_W] * xs[r0:r1, c0:c0 + SSD_GROUP_W])
            s_new = _bdot_tn(b_ms[c, g], xw[r0:r1, c0:c0 + SSD_GROUP_W])
            sstate[g] = st * yoff_scale[r1 - 1:r1, c0:c0 + SSD_GROUP_W] + s_new
        for hh in range(GDN_HEADS):
            k0 = hh * GDN_HEAD_K
            k1 = k0 + GDN_HEAD_K
            state = gstate[hh]
            lhs = jnp.concatenate([uws[c, hh][:, GDN_HEAD_V:], q_dec[r0:r1, k0:k1]], axis=0)
            ws_qs = _bdot(lhs, state)
            v_new = uws[c, hh][:, :GDN_HEAD_V] - ws_qs[:CHUNK]
            qk = kqs[c, hh][CHUNK:] * decs[c, hh]
            obuf[r0:r1, k0:k1] = ws_qs[CHUNK:] + _bdot(qk, v_new)
            gstate[hh] = state * egc_x[r1 - 1:r1, k0:k1] + _bdot_tn(k_dec[r0:r1, k0:k1], v_new)

    z = big_ref[:, BIG_Z:BIG_XBC]
    t = ybuf[...] * _silu(z)
    y_ssd = t * lax.rsqrt(jnp.mean(t * t, axis=-1, keepdims=True) + EPS) * snw_ref[...]
    o = obuf[...]
    gate = big_ref[:, BIG_GATE:BIG_COLS]
    parts = [y_ssd]
    for hh in range(GDN_HEADS):
        k0 = hh * GDN_HEAD_V
        o_h = o[:, k0:k0 + GDN_HEAD_V]
        parts.append(o_h * lax.rsqrt(jnp.mean(o_h * o_h, axis=-1, keepdims=True) + EPS)
                     * gnw_ref[...] * _silu(gate[:, k0:k0 + GDN_HEAD_V]))
    mixed = jnp.concatenate(parts, axis=1).astype(BF16)
    x1 = x_ref[...] + jnp.dot(mixed, wout_ref[...], preferred_element_type=F32)
    x1_ref[...] = x1
    hn2 = x1 * lax.rsqrt(jnp.mean(x1 * x1, axis=-1, keepdims=True) + EPS) * nfw_ref[...]
    hn2_ref[...] = hn2.astype(BF16)


def _mixer(x2, consts, batch, seq, rows):
    t = x2.shape[0]
    nblk = seq // rows
    tok = lambda b, s: (b * nblk + s, 0)
    const = lambda b, s: (0, 0)
    return pl.pallas_call(
        _mixer_kernel,
        grid=(batch, nblk),
        in_specs=[
            pl.BlockSpec((rows, D_MODEL), tok),
        ] + [pl.BlockSpec(a.shape, const) for a in consts],
        out_specs=[
            pl.BlockSpec((rows, D_MODEL), tok),
            pl.BlockSpec((rows, D_MODEL), tok),
        ],
        out_shape=[
            jax.ShapeDtypeStruct((t, D_MODEL), F32),
            jax.ShapeDtypeStruct((t, D_MODEL), BF16),
        ],
        scratch_shapes=[
            pltpu.VMEM((rows + CONV_PAD, SSD_CONV_CH), F32),
            pltpu.VMEM((rows + CONV_PAD, GDN_CONV_CH), F32),
            pltpu.VMEM((SSD_GROUPS, SSD_STATE, SSD_GROUP_W), F32),
            pltpu.VMEM((GDN_HEADS, GDN_HEAD_K, GDN_HEAD_V), F32),
            pltpu.VMEM((rows, SSD_WIDTH), F32),
            pltpu.VMEM((rows, GDN_WIDTH), F32),
            pltpu.VMEM((rows, BIG_COLS), F32),
        ],
        compiler_params=pltpu.CompilerParams(
            dimension_semantics=("arbitrary", "arbitrary"), vmem_limit_bytes=VMEM_LIMIT),
        name="mixer",
    )(x2, *consts)


def _batcher_network(n):
    def merge(lo, hi, r):
        step = r * 2
        if step < hi - lo:
            yield from merge(lo, hi, step)
            yield from merge(lo + r, hi, step)
            yield from [(i, i + r) for i in range(lo + r, hi - r, step)]
        else:
            yield (lo, lo + r)

    def sort(lo, hi):
        if hi - lo >= 1:
            mid = lo + (hi - lo) // 2
            yield from sort(lo, mid)
            yield from sort(mid + 1, hi)
            yield from merge(lo, hi, 1)

    return list(sort(0, n - 1))


def _top_desc_tiles(x, count):
    n_tiles = x.shape[0] // SUBLANES
    v = [x[k * SUBLANES:(k + 1) * SUBLANES] for k in range(n_tiles)]
    for i, j in _batcher_network(1 << (n_tiles - 1).bit_length()):
        if j < n_tiles:
            v[i], v[j] = jnp.maximum(v[i], v[j]), jnp.minimum(v[i], v[j])
    exhausted = jnp.float32(-jnp.inf)
    vals = []
    for r in range(count):
        m = jnp.max(v[0], axis=0, keepdims=True)
        vals.append(m)
        hit = v[0] == m
        for k in range(min(count - r - 1, n_tiles)):
            v[k] = jnp.where(hit, v[k + 1] if k + 1 < n_tiles else exhausted, v[k])
    return jnp.concatenate(vals, axis=0)


def _rank_products(a, b, sub8):
    cands = [b * a[0:1], b[0:8] * a[1:2]]
    for i in range(2, 8):
        cands.append(jnp.where(sub8 < (PEER_TOPK // (i + 1)), b[0:8] * a[i:i + 1], 0.0))
    cands.append(a[8:16] * b[0:1])
    return jnp.concatenate(cands, axis=0)


def _route_kernel(hn2_ref, wqt_ref, keys_ref, e1_ref, n1_ref, e2_ref, r2_ref, qt_scr):
    tn = hn2_ref.shape[0]
    qt_scr[...] = lax.dot_general(wqt_ref[...], hn2_ref[...], NT_DIMS, preferred_element_type=F32)
    sub8 = lax.broadcasted_iota(jnp.int32, (SUBLANES, LANES), 0)

    def head_body(h, carry):
        q0 = pl.multiple_of(h * PEER_DK, PEER_DK)
        s1_all = _bdot(keys_ref[0], qt_scr[pl.ds(q0, PEER_HALF), :])
        s2_all = _bdot(keys_ref[1], qt_scr[pl.ds(q0 + PEER_HALF, PEER_HALF), :])
        for lg in range(tn // LANES):
            l0 = lg * LANES
            s1 = s1_all[:, l0:l0 + LANES]
            s2 = s2_all[:, l0:l0 + LANES]
            x1 = jnp.exp(s1 - jnp.max(s1, axis=0, keepdims=True))
            x2 = jnp.exp(s2 - jnp.max(s2, axis=0, keepdims=True))
            a = _top_desc_tiles(x1, PEER_TOPK)
            b = _top_desc_tiles(x2, PEER_TOPK)
            cand = _rank_products(a, b, sub8)
            theta = _top_desc_tiles(cand, PEER_TOPK)[PEER_TOPK - 1:PEER_TOPK]
            chosen = cand >= theta
            zsum = jnp.sum(jnp.where(chosen, cand, 0.0), axis=0, keepdims=True)
            rz = 0.5 / zsum
            picked = jnp.where(chosen, 1.0, 0.0)
            counts = [jnp.sum(picked[0:PEER_TOPK], axis=0, keepdims=True)]
            for i in range(1, 8):
                r0 = PEER_TOPK + 8 * (i - 1)
                counts.append(jnp.sum(picked[r0:r0 + 8], axis=0, keepdims=True))
            n_rank = jnp.concatenate(counts + [picked[PEER_TOPK + 56:PEER_TOPK + 64]], axis=0)
            n1 = jnp.zeros_like(x1)
            r2 = jnp.full_like(x2, float(PEER_TOPK))
            for i in range(PEER_TOPK):
                n1 = jnp.where(x1 == a[i:i + 1], n_rank[i:i + 1], n1)
                r2 = jnp.where(x2 == b[i:i + 1], float(i), r2)
            e1_ref[h, :, l0:l0 + LANES] = x1 * rz
            n1_ref[h, :, l0:l0 + LANES] = n1
            e2_ref[h, :, l0:l0 + LANES] = x2.astype(BF16)
            r2_ref[h, :, l0:l0 + LANES] = r2.astype(BF16)
        return carry

    lax.fori_loop(0, PEER_HEADS, head_body, 0)


def _route(hn2, wqt, keys, tn):
    t = hn2.shape[0]
    sc_spec = pl.BlockSpec((PEER_HEADS, N_KEYS, tn), lambda i: (0, 0, i))
    sc_shape = jax.ShapeDtypeStruct((PEER_HEADS, N_KEYS, t), F32)
    sc_shape_bf = jax.ShapeDtypeStruct((PEER_HEADS, N_KEYS, t), BF16)
    return pl.pallas_call(
        _route_kernel,
        grid=(t // tn,),
        in_specs=[
            pl.BlockSpec((tn, D_MODEL), lambda i: (i, 0)),
            pl.BlockSpec(wqt.shape, lambda i: (0, 0)),
            pl.BlockSpec(keys.shape, lambda i: (0, 0, 0)),
        ],
        out_specs=[sc_spec, sc_spec, sc_spec, sc_spec],
        out_shape=[sc_shape, sc_shape, sc_shape_bf, sc_shape_bf],
        scratch_shapes=[pltpu.VMEM((PEER_HEADS * PEER_DK, tn), F32)],
        compiler_params=pltpu.CompilerParams(
            dimension_semantics=("arbitrary",), vmem_limit_bytes=VMEM_LIMIT),
        name="route",
    )(hn2, wqt, keys)


PEER_ROWS = 16


PEER_LANES = 256
MXU_N = 256
PEER_PIECE_ROWS = 256


def _peer_step(gate_tile, hn2t_scr, u_ref, vt_ref, e1_ref, n1_ref, e2_scr, r2_scr, yt_scr,
               e1b_scr, n1b_scr,
               act_write, act_read, w_write, w_read):
    te = u_ref.shape[0]
    tm = hn2t_scr.shape[1]
    n_e1 = te // N_KEYS

    def first_matmul(m, n):
        rows = slice(m * PEER_PIECE_ROWS, (m + 1) * PEER_PIECE_ROWS)
        act_write[rows, n * MXU_N:(n + 1) * MXU_N] = jnp.dot(
            u_ref[rows, :], hn2t_scr[:, n * MXU_N:(n + 1) * MXU_N], preferred_element_type=F32)

    def second_matmul(m, n):
        rows = slice(m * PEER_PIECE_ROWS, (m + 1) * PEER_PIECE_ROWS)
        yt_scr[rows, n * MXU_N:(n + 1) * MXU_N] += jnp.dot(
            vt_ref[rows, :], w_read[:, n * MXU_N:(n + 1) * MXU_N], preferred_element_type=F32)

    for e1l in range(n_e1):
        for h in range(PEER_HEADS):
            row = pl.ds(gate_tile * n_e1 + e1l, 1)
            e1b_scr[e1l, h] = jnp.broadcast_to(e1_ref[h, row, :], (PEER_ROWS, tm)).astype(BF16)
            n1b_scr[e1l, h] = jnp.broadcast_to(n1_ref[h, row, :], (PEER_ROWS, tm)).astype(BF16)

    def gates(e1l, lb):
        n_groups = N_KEYS // PEER_ROWS
        lanes = slice(lb * PEER_LANES, (lb + 1) * PEER_LANES)
        accs = [None] * n_groups
        zero = jnp.zeros((PEER_ROWS, PEER_LANES), BF16)
        for h in range(PEER_HEADS):
            e1b = e1b_scr[e1l, h, :, lanes]
            n1b = n1b_scr[e1l, h, :, lanes]
            for r in range(n_groups):
                rows = slice(r * PEER_ROWS, (r + 1) * PEER_ROWS)
                sel = jnp.where(r2_scr[h, rows, lanes] < n1b, e2_scr[h, rows, lanes] * e1b, zero)
                accs[r] = sel if accs[r] is None else accs[r] + sel
        for r in range(n_groups):
            rows = slice(e1l * N_KEYS + r * PEER_ROWS, e1l * N_KEYS + (r + 1) * PEER_ROWS)
            a = act_read[rows, lanes]
            w_write[rows, lanes] = accs[r] * (a * (1.0 + lax.erf(a * (2.0 ** -0.5)))).astype(BF16)

    pieces = ([functools.partial(first_matmul, m, n)
               for n in range(tm // MXU_N) for m in range(te // PEER_PIECE_ROWS)]
              + [functools.partial(second_matmul, m, n)
                 for n in range(tm // MXU_N) for m in range(D_MODEL // PEER_PIECE_ROWS)])
    blocks = [(e1l, lb) for e1l in range(n_e1) for lb in range(tm // PEER_LANES)]
    per_block = -(-len(pieces) // len(blocks))
    for i, (e1l, lb) in enumerate(blocks):
        for piece in pieces[i * per_block:(i + 1) * per_block]:
            piece()
        gates(e1l, lb)
    for piece in pieces[len(blocks) * per_block:]:
        piece()


def _peer_kernel(hn2_ref, u_ref, vt_ref, e1_ref, n1_ref, e2_ref, r2_ref, x1_ref, nw_ref,
                 out_ref, yt_scr, act0_scr, act1_scr, w0_scr, w1_scr, e1b_scr, n1b_scr, e2_scr, r2_scr,
                 hn2t_scr,
                 *, n_tiles, n_steps):
    s = pl.program_id(0)
    tm = hn2_ref.shape[0]
    gate_tile = jnp.clip(s - 1, 0, n_steps - 1) % n_tiles
    lag_tile = jnp.clip(s - 2, 0, n_steps - 1) % n_tiles

    @pl.when(s == 0)
    def _():
        act1_scr[...] = jnp.zeros(act1_scr.shape, F32)
        w1_scr[...] = jnp.zeros(w1_scr.shape, BF16)

    @pl.when(lag_tile == 0)
    def _():
        yt_scr[...] = jnp.zeros(yt_scr.shape, F32)

    @pl.when(gate_tile == 0)
    def _():
        for h in range(PEER_HEADS):
            e2_scr[h] = e2_ref[h]
            r2_scr[h] = r2_ref[h]

    @pl.when(jnp.minimum(s, n_steps - 1) % n_tiles == 0)
    def _():
        hn2t_scr[...] = hn2_ref[...].astype(F32).T.astype(BF16)

    args = (gate_tile, hn2t_scr, u_ref, vt_ref, e1_ref, n1_ref, e2_scr, r2_scr, yt_scr,
            e1b_scr, n1b_scr)

    @pl.when(s % 2 == 0)
    def _():
        _peer_step(*args, act0_scr, act1_scr, w0_scr, w1_scr)

    @pl.when(s % 2 == 1)
    def _():
        _peer_step(*args, act1_scr, act0_scr, w1_scr, w0_scr)

    @pl.when((lag_tile == n_tiles - 1) & (s >= 2))
    def _():
        x = x1_ref[...] + yt_scr[...].T
        out_ref[...] = x * lax.rsqrt(jnp.mean(x * x, axis=-1, keepdims=True) + EPS) * nw_ref[...]


def _peer(hn2, u_bf, vt_bf, e1, n1, e2, r2, x1, nw, tm, te):
    t = hn2.shape[0]
    n_tiles = N_EXPERTS // te
    n_steps = (t // tm) * n_tiles
    assert tm % MXU_N == 0 and tm % PEER_LANES == 0 and te % PEER_PIECE_ROWS == 0
    tile_a = lambda s: jnp.minimum(s, n_steps - 1)
    tile_b = lambda s: jnp.clip(s - 1, 0, n_steps - 1)
    tile_c = lambda s: jnp.clip(s - 2, 0, n_steps - 1)
    sc_spec = pl.BlockSpec((PEER_HEADS, N_KEYS, tm), lambda s: (0, 0, tile_b(s) // n_tiles))
    return pl.pallas_call(
        functools.partial(_peer_kernel, n_tiles=n_tiles, n_steps=n_steps),
        grid=(n_steps + 2,),
        in_specs=[
            pl.BlockSpec((tm, D_MODEL), lambda s: (tile_a(s) // n_tiles, 0)),
            pl.BlockSpec((te, D_MODEL), lambda s: (tile_a(s) % n_tiles, 0)),
            pl.BlockSpec((None, D_MODEL, te), lambda s: (tile_c(s) % n_tiles, 0, 0)),
            sc_spec, sc_spec, sc_spec, sc_spec,
            pl.BlockSpec((tm, D_MODEL), lambda s: (tile_c(s) // n_tiles, 0)),
            pl.BlockSpec((1, D_MODEL), lambda s: (0, 0)),
        ],
        out_specs=pl.BlockSpec((tm, D_MODEL), lambda s: (tile_c(s) // n_tiles, 0)),
        out_shape=jax.ShapeDtypeStruct((t, D_MODEL), F32),
        scratch_shapes=[
            pltpu.VMEM((D_MODEL, tm), F32),
            pltpu.VMEM((te, tm), F32),
            pltpu.VMEM((te, tm), F32),
            pltpu.VMEM((te, tm), BF16),
            pltpu.VMEM((te, tm), BF16),
            pltpu.VMEM((te // N_KEYS, PEER_HEADS, PEER_ROWS, tm), BF16),
            pltpu.VMEM((te // N_KEYS, PEER_HEADS, PEER_ROWS, tm), BF16),
            pltpu.VMEM((PEER_HEADS, N_KEYS, tm), BF16),
            pltpu.VMEM((PEER_HEADS, N_KEYS, tm), BF16),
            pltpu.VMEM((D_MODEL, tm), BF16),
        ],
        compiler_params=pltpu.CompilerParams(
            dimension_semantics=("arbitrary",), vmem_limit_bytes=VMEM_LIMIT),
        name="peer",
    )(hn2, u_bf, vt_bf, e1, n1, e2, r2, x1, nw)


def _pick(n, pref):
    t = min(pref, n)
    while n % t:
        t //= 2
    return t


def _expansion(src_col0, n_heads, width):
    r = jnp.arange(LANES)[:, None]
    c = jnp.arange(n_heads * width)[None, :]
    return (r == src_col0 + c // width).astype(BF16)


def _chunk_masks(rows):
    r = jnp.arange(rows)[:, None]
    c = jnp.arange(rows)[None, :]
    same = (r // CHUNK) == (c // CHUNK)
    return ((same & (r >= c)).astype(BF16), (same & (r <= c)).astype(BF16), same.astype(BF16))


TILE_MIXER = 256
TILE_ROUTE = 512
TILE_PEER_TOKENS = 512
TILE_PEER_EXPERTS = 1024


def kernel(x, norm_mix_w, w_in, ssd_conv_w, ssd_conv_b, ssd_dt_bias, ssd_a_log, ssd_d, ssd_norm_w,
           gdn_conv_w, gdn_dt_bias, gdn_a_log, gdn_norm_w, w_out, norm_ffn_w, peer_w_q, peer_sub_keys,
           peer_u, peer_v, norm_final_w):
    batch, seq, _ = x.shape
    assert w_in.shape[0] == 1, "single trunk layer"
    t = batch * seq
    x2 = x.reshape(t, D_MODEL)
    w_in = w_in[0]

    wbig = jnp.concatenate([w_in[:, OFF_SSD_Z:OFF_SSD_DT], w_in[:, OFF_GDN_QKV:OFF_GDN_BETA]],
                           axis=1).astype(BF16)
    wsm = jnp.concatenate([w_in[:, OFF_SSD_DT:OFF_GDN_QKV], w_in[:, OFF_GDN_BETA:IN_COLS]], axis=1)
    wsm = jnp.pad(wsm, ((0, 0), (0, LANES - wsm.shape[1]))).astype(BF16)
    zeros4 = jnp.zeros((GDN_HEADS,), F32)
    bias = jnp.concatenate([ssd_dt_bias[0].astype(F32), zeros4, gdn_dt_bias[0].astype(F32)])
    amul = jnp.concatenate([-jnp.exp(ssd_a_log[0].astype(F32)), zeros4, -jnp.exp(gdn_a_log[0].astype(F32))])
    pad = LANES - bias.shape[0]
    prow = jnp.pad(jnp.stack([bias, amul]), ((0, SUBLANES - 2), (0, pad)))
    pcol = jnp.pad(jnp.stack([bias, amul], axis=1), ((0, pad), (0, LANES - 2)))
    rows = _pick(seq, TILE_MIXER)
    tril, triu, blk = _chunk_masks(rows)
    consts = [
        norm_mix_w[0][None, :].astype(F32), wbig, wsm, wsm.T,
        ssd_conv_w[0].astype(F32), ssd_conv_b[0][None, :].astype(F32), gdn_conv_w[0].astype(F32),
        prow, pcol, tril, triu, blk,
        _expansion(SM_DT, SSD_HEADS, SSD_HEAD_DIM),
        _expansion(SM_BETA, GDN_HEADS, GDN_HEAD_V),
        _expansion(SM_ALPHA, GDN_HEADS, GDN_HEAD_V),
        jnp.repeat(ssd_d[0].astype(F32), SSD_HEAD_DIM)[None, :],
        ssd_norm_w[0][None, :].astype(F32), gdn_norm_w[0][None, :].astype(F32),
        w_out[0].astype(BF16), norm_ffn_w[0][None, :].astype(F32),
    ]

    x1, hn2 = _mixer(x2, consts, batch, seq, rows)
    e1, n1, e2, r2 = _route(hn2, peer_w_q[0].T.astype(BF16), peer_sub_keys[0].astype(BF16),
                        _pick(t, TILE_ROUTE))
    vt_tiles = peer_v[0].astype(BF16).reshape(N_EXPERTS // TILE_PEER_EXPERTS, TILE_PEER_EXPERTS,
                                               D_MODEL).transpose(0, 2, 1)
    out = _peer(hn2, peer_u[0].astype(BF16), vt_tiles, e1, n1, e2, r2, x1,
                norm_final_w[None, :].astype(F32), _pick(t, TILE_PEER_TOKENS), TILE_PEER_EXPERTS)
    return out.reshape(batch, seq, D_MODEL)
```
